```python
import jax, jax.numpy as jnp
from jax import lax
import numpy as np

D_MODEL = 1024
BATCH = 2
SEQ = 8192
DEPTH = 1
DEC_BATCH = 32
DEC_SEQ = 1
PAST_LEN = 16384
PAGE_SIZE = 128

N_HEADS = 8
HEAD_DIM = 64
ATT_WIDTH = N_HEADS * HEAD_DIM
POOL_WINDOWS = (2, 4, 8, 16)
N_POOL_GROUPS = len(POOL_WINDOWS)
POOL_WIDTH = D_MODEL // 2
POOL_GROUP = POOL_WIDTH // N_POOL_GROUPS
POOL_STATE = max(POOL_WINDOWS) - 1
MOBA_BLOCK = 256
MOBA_TOPK = 3
Q_CHUNK = 64
ROPE_THETA = 10000.0
PLE_DIM = 256
LN_EPS = 1e-5
DEEPNORM_ALPHA = (2 * DEPTH) ** 0.25
DEEPNORM_BETA = (8 * DEPTH) ** -0.25
IN_SPLITS = (ATT_WIDTH, ATT_WIDTH, ATT_WIDTH, ATT_WIDTH, POOL_WIDTH, POOL_WIDTH, D_MODEL, D_MODEL)
IN_WIDTH = sum(IN_SPLITS)

kernel_name = 'hybrid_pool_moba_decode_step'


def _layer_norm(x, g, b):
    xf = x.astype(jnp.float32)
    mu = jnp.mean(xf, axis=-1, keepdims=True)
    var = jnp.mean(jnp.square(xf - mu), axis=-1, keepdims=True)
    return ((xf - mu) * lax.rsqrt(var + LN_EPS) * g + b).astype(x.dtype)


def _rope(x, pos):
    half = HEAD_DIM // 2
    inv = ROPE_THETA ** (-jnp.arange(half, dtype=jnp.float32) / half)
    ang = pos.astype(jnp.float32)[:, None] * inv[None, :]
    cos = jnp.cos(ang)[None, :, None, :]
    sin = jnp.sin(ang)[None, :, None, :]
    xf = x.astype(jnp.float32)
    x1, x2 = xf[..., :half], xf[..., half:]
    return jnp.concatenate([x1 * cos - x2 * sin, x2 * cos + x1 * sin], axis=-1).astype(x.dtype)


def _mixer_inputs(x, w_in, pos):
    b, n, _ = x.shape
    offs = np.cumsum(IN_SPLITS)[:-1].tolist()
    q, k, v, zb, u, za, ga, gb = jnp.split(jnp.einsum('bnd,de->bne', x, w_in), offs, axis=-1)
    q = _rope(q.reshape(b, n, N_HEADS, HEAD_DIM), pos)
    k = _rope(k.reshape(b, n, N_HEADS, HEAD_DIM), pos)
    v = v.reshape(b, n, N_HEADS, HEAD_DIM)
    return q, k, v, zb, u, za, ga, gb


def _pool_mix(u_hist, u, pos0, w_mix, scale):
    b, n, c = u.shape
    ext = jnp.concatenate([u_hist, u], axis=1).astype(jnp.float32)
    csum = jnp.cumsum(jnp.concatenate([jnp.zeros((b, 1, c), jnp.float32), ext], axis=1), axis=1)
    end = csum[:, POOL_STATE + 1:]
    cur = ext[:, POOL_STATE:]
    pos = pos0 + jnp.arange(n)
    outs = []
    for g, w in enumerate(POOL_WINDOWS):
        sl = slice(g * POOL_GROUP, (g + 1) * POOL_GROUP)
        start = csum[:, POOL_STATE + 1 - w: POOL_STATE + 1 - w + n, sl]
        cnt = jnp.minimum(pos + 1, w).astype(jnp.float32)[None, :, None]
        diff = (end[..., sl] - start) / cnt - cur[..., sl]
        outs.append(jnp.einsum('bnc,ce->bne', diff, w_mix[g].astype(jnp.float32)))
    y = jnp.concatenate(outs, axis=-1) * scale.astype(jnp.float32)
    return y.astype(u.dtype)


def _moba_select(q, kmean, qpos):
    nb = kmean.shape[2]
    ksel = min(MOBA_TOPK, nb)
    s = jnp.einsum('bhqd,bhjd->bhqj', q.astype(jnp.float32), kmean.astype(jnp.float32))
    qblk = qpos // MOBA_BLOCK
    past = jnp.arange(nb)[None, :] < qblk[:, None]
    s = jnp.where(past[None, None], s, -jnp.inf)
    _, top = lax.top_k(s, ksel)
    own = jnp.broadcast_to(qblk[None, None, :, None], top.shape[:3] + (1,))
    blocks = jnp.concatenate([top, own], axis=-1).astype(jnp.int32)
    valid_sel = jnp.arange(ksel)[None, :] < qblk[:, None]
    valid = jnp.concatenate([valid_sel, jnp.ones((qpos.shape[0], 1), bool)], axis=-1)
    return blocks, valid


def _moba_attend(q, kg, vg, blocks, valid, qpos):
    key_pos = blocks[..., None] * MOBA_BLOCK + jnp.arange(MOBA_BLOCK)
    mask = valid[None, None, :, :, None] & (key_pos <= qpos[None, None, :, None, None])
    logits = jnp.einsum('bhqd,bhqjkd->bhqjk', q.astype(jnp.float32), kg.astype(jnp.float32)) * (HEAD_DIM ** -0.5)
    logits = jnp.where(mask, logits, -jnp.inf)
    b, h, n, j, kk = logits.shape
    probs = jax.nn.softmax(logits.reshape(b, h, n, j * kk), axis=-1).reshape(b, h, n, j, kk)
    out = jnp.einsum('bhqjk,bhqjkd->bhqd', probs, vg.astype(jnp.float32))
    return out.astype(q.dtype)


def _moba_prompt(q, k, v):
    b, s = q.shape[:2]
    nb = -(-s // MOBA_BLOCK)
    pad = ((0, 0), (0, nb * MOBA_BLOCK - s), (0, 0), (0, 0))
    kb = jnp.pad(k, pad).reshape(b, nb, MOBA_BLOCK, N_HEADS, HEAD_DIM).transpose(0, 3, 1, 2, 4)
    vb = jnp.pad(v, pad).reshape(b, nb, MOBA_BLOCK, N_HEADS, HEAD_DIM).transpose(0, 3, 1, 2, 4)
    kmean = jnp.sum(kb.astype(jnp.float32), axis=3) / MOBA_BLOCK
    qh = q.transpose(0, 2, 1, 3)
    bidx = jnp.arange(b)[:, None, None, None]
    hidx = jnp.arange(N_HEADS)[None, :, None, None]

    def chunk(c):
        qpos = c * Q_CHUNK + jnp.arange(Q_CHUNK)
        qc = lax.dynamic_slice_in_dim(qh, c * Q_CHUNK, Q_CHUNK, axis=2)
        blocks, valid = _moba_select(qc, kmean, qpos)
        kg = kb[bidx, hidx, blocks]
        vg = vb[bidx, hidx, blocks]
        return _moba_attend(qc, kg, vg, blocks, valid, qpos)

    out = lax.map(chunk, jnp.arange(s // Q_CHUNK))
    return out.transpose(1, 0, 3, 2, 4).reshape(b, s, ATT_WIDTH)


def _moba_sample(q, k, v, cache_k, cache_v, page_table):
    db, n = q.shape[:2]
    n_pages = page_table.shape[1]
    ppb = MOBA_BLOCK // PAGE_SIZE
    nnp = -(-n // PAGE_SIZE)
    pad = ((0, 0), (0, nnp * PAGE_SIZE - n), (0, 0), (0, 0))
    k_new = jnp.pad(k, pad).reshape(db, nnp, PAGE_SIZE, N_HEADS, HEAD_DIM)
    v_new = jnp.pad(v, pad).reshape(db, nnp, PAGE_SIZE, N_HEADS, HEAD_DIM)
    ps_cache = jnp.sum(cache_k[page_table].astype(jnp.float32), axis=2)
    ps_new = jnp.sum(k_new.astype(jnp.float32), axis=2)
    ps = jnp.concatenate([ps_cache, ps_new], axis=1)
    n_lp = n_pages + nnp
    nb = -(-n_lp // ppb)
    ps = jnp.pad(ps, ((0, 0), (0, nb * ppb - n_lp), (0, 0), (0, 0)))
    kmean = ps.reshape(db, nb, ppb, N_HEADS, HEAD_DIM).sum(2).transpose(0, 2, 1, 3) / MOBA_BLOCK
    qpos = PAST_LEN + jnp.arange(n)
    qh = q.transpose(0, 2, 1, 3)
    blocks, valid = _moba_select(qh, kmean, qpos)
    lp = blocks[..., None] * ppb + jnp.arange(ppb)
    in_cache = lp < n_pages
    bidx = jnp.arange(db)[:, None, None, None, None]
    hidx = jnp.arange(N_HEADS)[None, :, None, None, None]
    phys = page_table[bidx, jnp.clip(lp, 0, n_pages - 1)]
    newp = jnp.clip(lp - n_pages, 0, nnp - 1)
    rows = jnp.arange(PAGE_SIZE)

    def gather(pool, new_rows):
        from_cache = pool[phys[..., None], rows, hidx[..., None]]
        from_new = new_rows[bidx[..., None], newp[..., None], rows, hidx[..., None]]
        sel = jnp.where(in_cache[..., None, None], from_cache, from_new)
        return sel.reshape(db, N_HEADS, n, blocks.shape[-1], MOBA_BLOCK, HEAD_DIM)

    out = _moba_attend(qh, gather(cache_k, k_new), gather(cache_v, v_new), blocks, valid, qpos)
    return out.transpose(0, 2, 1, 3).reshape(db, n, ATT_WIDTH)


def _layer_out(x, att, zb, pool, za, ga, gb, p, w_pool_out, w_att_out, w_o, ln_g, ln_b, w_ple, w_ple_gate):
    y_pool = jnp.einsum('bnc,cd->bnd', pool * jax.nn.silu(za), w_pool_out)
    y_att = jnp.einsum('bnc,cd->bnd', att * jax.nn.silu(zb), w_att_out)
    merged = jax.nn.sigmoid(ga) * y_pool + jax.nn.sigmoid(gb) * y_att
    h = _layer_norm(DEEPNORM_ALPHA * x + jnp.einsum('bnc,cd->bnd', merged, w_o), ln_g, ln_b)
    gate = jax.nn.sigmoid(jnp.einsum('bnd,de->bne', h, w_ple_gate))
    return h + gate * jnp.einsum('bnp,pd->bnd', p, w_ple)


def setup_inputs(seed: int = 0) -> dict:
    key = jax.random.key(seed)
    ks = jax.random.split(key, 20)
    n_pages = PAST_LEN // PAGE_SIZE
    n_used = DEC_BATCH * n_pages
    n_phys = n_used + n_used // 4
    page_table = jax.random.permutation(ks[0], n_phys)[:n_used].reshape(DEC_BATCH, n_pages).astype(jnp.int32)
    f32 = jnp.float32
    col_scale = np.ones((IN_WIDTH,), np.float32)
    v_off = 2 * ATT_WIDTH
    u_off = 4 * ATT_WIDTH
    col_scale[v_off:v_off + ATT_WIDTH] = DEEPNORM_BETA
    col_scale[u_off:u_off + POOL_WIDTH] = DEEPNORM_BETA
    w_in = jax.random.normal(ks[1], (DEPTH, D_MODEL, IN_WIDTH), f32) * (D_MODEL ** -0.5) * jnp.asarray(col_scale)
    return {
        'x_prompt': jax.random.normal(ks[2], (BATCH, SEQ, D_MODEL), f32),
        'x_sample': jax.random.normal(ks[3], (DEC_BATCH, DEC_SEQ, D_MODEL), f32),
        'cache_k': jax.random.normal(ks[4], (DEPTH, n_phys, PAGE_SIZE, N_HEADS, HEAD_DIM), f32),
        'cache_v': jax.random.normal(ks[5], (DEPTH, n_phys, PAGE_SIZE, N_HEADS, HEAD_DIM), f32) * DEEPNORM_BETA,
        'state_pool': jax.random.normal(ks[6], (DEPTH, DEC_BATCH, POOL_STATE, POOL_WIDTH), f32) * DEEPNORM_BETA,
        'page_table': page_table,
        'p_prompt': jax.random.normal(ks[7], (DEPTH, BATCH, SEQ, PLE_DIM), f32),
        'p_sample': jax.random.normal(ks[8], (DEPTH, DEC_BATCH, DEC_SEQ, PLE_DIM), f32),
        'w_in': w_in,
        'w_pool_mix': jax.random.normal(ks[9], (DEPTH, N_POOL_GROUPS, POOL_GROUP, POOL_GROUP), f32) * (POOL_GROUP ** -0.5),
        'pool_scale': 1.0 + 0.1 * jax.random.normal(ks[10], (DEPTH, POOL_WIDTH), f32),
        'w_pool_out': jax.random.normal(ks[11], (DEPTH, POOL_WIDTH, D_MODEL), f32) * (POOL_WIDTH ** -0.5) * DEEPNORM_BETA,
        'w_att_out': jax.random.normal(ks[12], (DEPTH, ATT_WIDTH, D_MODEL), f32) * (ATT_WIDTH ** -0.5) * DEEPNORM_BETA,
        'w_o': jax.random.normal(ks[13], (DEPTH, D_MODEL, D_MODEL), f32) * (D_MODEL ** -0.5) * DEEPNORM_BETA,
        'ln_g': 1.0 + 0.1 * jax.random.normal(ks[14], (DEPTH, D_MODEL), f32),
        'ln_b': 0.1 * jax.random.normal(ks[15], (DEPTH, D_MODEL), f32),
        'w_ple': jax.random.normal(ks[16], (DEPTH, PLE_DIM, D_MODEL), f32) * (PLE_DIM ** -0.5),
        'w_ple_gate': jax.random.normal(ks[17], (DEPTH, D_MODEL, D_MODEL), f32) * (D_MODEL ** -0.5),
    }


def reference(x_prompt, x_sample, cache_k, cache_v, state_pool, page_table, p_prompt, p_sample,
              w_in, w_pool_mix, pool_scale, w_pool_out, w_att_out, w_o, ln_g, ln_b, w_ple, w_ple_gate):
    xp, xs = x_prompt, x_sample
    b, s, _ = xp.shape
    db, n, _ = xs.shape
    pos_p = jnp.arange(s)
    pos_s = PAST_LEN + jnp.arange(n)
    kp_l, vp_l, pp_l, ks_l, vs_l, psm_l = [], [], [], [], [], []
    for i in range(DEPTH):
        q, k, v, zb, u, za, ga, gb = _mixer_inputs(xp, w_in[i], pos_p)
        att = _moba_prompt(q, k, v)
        hist = jnp.zeros((b, POOL_STATE, POOL_WIDTH), u.dtype)
        pool = _pool_mix(hist, u, 0, w_pool_mix[i], pool_scale[i])
        kp_l.append(k)
        vp_l.append(v)
        pp_l.append(jnp.concatenate([hist, u], axis=1)[:, -POOL_STATE:])
        xp = _layer_out(xp, att, zb, pool, za, ga, gb, p_prompt[i], w_pool_out[i], w_att_out[i], w_o[i],
                        ln_g[i], ln_b[i], w_ple[i], w_ple_gate[i])
        q, k, v, zb, u, za, ga, gb = _mixer_inputs(xs, w_in[i], pos_s)
        att = _moba_sample(q, k, v, cache_k[i], cache_v[i], page_table)
        hist = state_pool[i].astype(u.dtype)
        pool = _pool_mix(hist, u, PAST_LEN, w_pool_mix[i], pool_scale[i])
        ks_l.append(k)
        vs_l.append(v)
        psm_l.append(jnp.concatenate([hist, u], axis=1)[:, -POOL_STATE:])
        xs = _layer_out(xs, att, zb, pool, za, ga, gb, p_sample[i], w_pool_out[i], w_att_out[i], w_o[i],
                        ln_g[i], ln_b[i], w_ple[i], w_ple_gate[i])
    k_prompt = jnp.stack(kp_l)
    v_prompt = jnp.stack(vp_l)
    pool_prompt = jnp.stack(pp_l)
    k_sample = jnp.stack(ks_l)
    v_sample = jnp.stack(vs_l)
    pool_sample = jnp.stack(psm_l)
    return (xp, xs, k_prompt, v_prompt, pool_prompt, k_sample, v_sample, pool_sample)
```

```python
import functools

import numpy as np
import jax
import jax.numpy as jnp
from jax import lax
from jax.experimental import pallas as pl
from jax.experimental.pallas import tpu as pltpu

D_MODEL = 1024
N_HEADS = 8
HEAD_DIM = 64
HALF_DIM = HEAD_DIM // 2
ATT_WIDTH = N_HEADS * HEAD_DIM
POOL_WINDOWS = (2, 4, 8, 16)
POOL_WIDTH = D_MODEL // 2
POOL_GROUP = POOL_WIDTH // len(POOL_WINDOWS)
POOL_STATE = max(POOL_WINDOWS) - 1
MOBA_BLOCK = 256
MOBA_TOPK = 3
PAGE_SIZE = 128
PAGES_PER_BLOCK = MOBA_BLOCK // PAGE_SIZE
ROPE_THETA = 10000.0
PLE_DIM = 256
LN_EPS = 1e-5
DEPTH = 1
DEEPNORM_ALPHA = (2 * DEPTH) ** 0.25
IN_SPLITS = (ATT_WIDTH, ATT_WIDTH, ATT_WIDTH, ATT_WIDTH, POOL_WIDTH, POOL_WIDTH, D_MODEL, D_MODEL)
IN_OFFS = tuple(int(v) for v in np.cumsum((0,) + IN_SPLITS))
IN_WIDTH = IN_OFFS[-1]
Q_SCALE = HEAD_DIM ** -0.5

LANES = 128
HIST_ROWS = 16
MASK_NEG = -1e30
ROW_TILE = 256
PAGES_PER_STEP = 16
VMEM_LIMIT = 52 * 1024 * 1024

F32 = jnp.float32
BF16 = jnp.bfloat16


def _sigmoid(x):
    return 1.0 / (1.0 + jnp.exp(-x))


def _silu(x):
    return x * _sigmoid(x)


def _dot(a, b):
    return jnp.dot(a, b, preferred_element_type=F32)


def _rope(t, cos, sin_signed):
    lane = lax.broadcasted_iota(jnp.int32, (1, LANES), 1)
    first_half = (lane % HEAD_DIM) < HALF_DIM
    outs = []
    for c in range(t.shape[1] // LANES):
        tc = t[:, c * LANES:(c + 1) * LANES]
        partner = jnp.where(first_half, pltpu.roll(tc, LANES - HALF_DIM, 1), pltpu.roll(tc, HALF_DIM, 1))
        outs.append(tc * cos + partner * sin_signed)
    return jnp.concatenate(outs, axis=1)


def _project(xb, w_ref, seg):
    return _dot(xb, w_ref[:, IN_OFFS[seg]:IN_OFFS[seg + 1]])


def _prompt_inproj_kernel(x_ref, w_ref, cos_ref, sin_ref,
                          qT_ref, kx_ref, vT_ref, k_ref, v_ref, km_ref,
                          zb_ref, u_ref, za_ref, ga_ref, gb_ref):
    i = pl.program_id(1)
    xb = x_ref[...].astype(BF16)
    cos = cos_ref[...]
    sin = sin_ref[...]

    q = _rope(_project(xb, w_ref, 0), cos, sin) * Q_SCALE
    qT_ref[...] = q.T.astype(BF16)

    k = _rope(_project(xb, w_ref, 1), cos, sin)
    k_ref[...] = k
    km_ref[...] = jnp.sum(k, axis=0, keepdims=True) / MOBA_BLOCK
    lane = lax.broadcasted_iota(jnp.int32, (ROW_TILE, LANES), 1)
    extra = jnp.where(lane - HEAD_DIM == i, 1.0, 0.0)
    for h in range(N_HEADS):
        pair = k[:, (h // 2) * LANES:(h // 2 + 1) * LANES]
        base = pair if h % 2 == 0 else pltpu.roll(pair, HEAD_DIM, 1)
        kx_ref[h, 0] = jnp.where(lane < HEAD_DIM, base, extra).astype(BF16)

    v = _project(xb, w_ref, 2)
    v_ref[...] = v
    vT_ref[:, 0] = v.T.astype(BF16).reshape(N_HEADS, HEAD_DIM, ROW_TILE)

    zb_ref[...] = _project(xb, w_ref, 3)
    u_ref[...] = _project(xb, w_ref, 4)
    za_ref[...] = _project(xb, w_ref, 5)
    ga_ref[...] = _project(xb, w_ref, 6)
    gb_ref[...] = _project(xb, w_ref, 7)


def _prompt_inproj(x, w_in, cos, sin):
    b, s, _ = x.shape
    nb = s // MOBA_BLOCK
    assert ROW_TILE == MOBA_BLOCK and s % ROW_TILE == 0 and nb <= 32
    row = lambda width: pl.BlockSpec((None, ROW_TILE, width), lambda bi, i: (bi, i, 0))
    out_shape = [
        jax.ShapeDtypeStruct((b, ATT_WIDTH, s), BF16),
        jax.ShapeDtypeStruct((b, N_HEADS, nb, MOBA_BLOCK, LANES), BF16),
        jax.ShapeDtypeStruct((b, N_HEADS, nb, HEAD_DIM, MOBA_BLOCK), BF16),
        jax.ShapeDtypeStruct((b, s, ATT_WIDTH), F32),
        jax.ShapeDtypeStruct((b, s, ATT_WIDTH), F32),
        jax.ShapeDtypeStruct((b, nb, 1, ATT_WIDTH), F32),
        jax.ShapeDtypeStruct((b, s, ATT_WIDTH), F32),
        jax.ShapeDtypeStruct((b, s, POOL_WIDTH), F32),
        jax.ShapeDtypeStruct((b, s, POOL_WIDTH), F32),
        jax.ShapeDtypeStruct((b, s, D_MODEL), F32),
        jax.ShapeDtypeStruct((b, s, D_MODEL), F32),
    ]
    out_specs = [
        pl.BlockSpec((None, ATT_WIDTH, ROW_TILE), lambda bi, i: (bi, 0, i)),
        pl.BlockSpec((None, N_HEADS, 1, MOBA_BLOCK, LANES), lambda bi, i: (bi, 0, i, 0, 0)),
        pl.BlockSpec((None, N_HEADS, 1, HEAD_DIM, MOBA_BLOCK), lambda bi, i: (bi, 0, i, 0, 0)),
        row(ATT_WIDTH), row(ATT_WIDTH),
        pl.BlockSpec((None, None, 1, ATT_WIDTH), lambda bi, i: (bi, i, 0, 0)),
        row(ATT_WIDTH), row(POOL_WIDTH), row(POOL_WIDTH), row(D_MODEL), row(D_MODEL),
    ]
    in_specs = [
        row(D_MODEL),
        pl.BlockSpec((D_MODEL, IN_WIDTH), lambda bi, i: (0, 0)),
        pl.BlockSpec((ROW_TILE, LANES), lambda bi, i: (i, 0)),
        pl.BlockSpec((ROW_TILE, LANES), lambda bi, i: (i, 0)),
    ]
    return pl.pallas_call(
        _prompt_inproj_kernel,
        grid=(b, s // ROW_TILE),
        in_specs=in_specs, out_specs=out_specs, out_shape=out_shape,
        compiler_params=pltpu.CompilerParams(
            dimension_semantics=("arbitrary", "arbitrary"), vmem_limit_bytes=VMEM_LIMIT),
        name="prompt_inproj",
    )(x, w_in, cos, sin)


def _select_bias(scores, n_past, own):
    blk = lax.broadcasted_iota(jnp.int32, scores.shape, 0)
    n_blk = scores.shape[0]
    s = jnp.where(blk < n_past, scores, -jnp.inf)
    chosen = blk == own
    for _ in range(MOBA_TOPK):
        m = jnp.max(s, axis=0, keepdims=True)
        cand = jnp.where((s == m) & (m > -jnp.inf), blk, n_blk)
        first = jnp.min(cand, axis=0, keepdims=True)
        pick = blk == first
        chosen = chosen | pick
        s = jnp.where(pick, -jnp.inf, s)
    return jnp.where(chosen, 0.0, MASK_NEG)


def _prompt_attn_kernel(qT_ref, km_ref, kx_ref, vT_ref, o_ref):
    i = pl.program_id(2)
    qT = qT_ref[...]
    n_blk = km_ref.shape[0]
    sel_scores = _dot(km_ref[...].astype(BF16), qT)
    bias = _select_bias(sel_scores, i, i)
    pad = jnp.zeros((LANES - HEAD_DIM - n_blk, MOBA_BLOCK), BF16)
    qx = jnp.concatenate([qT, bias.astype(BF16), pad], axis=0)

    key_row = lax.broadcasted_iota(jnp.int32, (MOBA_BLOCK, MOBA_BLOCK), 0)
    qry_col = lax.broadcasted_iota(jnp.int32, (MOBA_BLOCK, MOBA_BLOCK), 1)
    s = jnp.where(key_row <= qry_col, _dot(kx_ref[i], qx), MASK_NEG)
    m = jnp.max(s, axis=0, keepdims=True)
    p = jnp.exp(s - m)
    l = jnp.sum(p, axis=0, keepdims=True)
    acc = _dot(vT_ref[i], p.astype(BF16))

    def body(j, carry):
        m, l, acc = carry
        s = _dot(kx_ref[j], qx)
        m_new = jnp.maximum(m, jnp.max(s, axis=0, keepdims=True))
        alpha = jnp.exp(m - m_new)
        p = jnp.exp(s - m_new)
        l = alpha * l + jnp.sum(p, axis=0, keepdims=True)
        acc = alpha * acc + _dot(vT_ref[j], p.astype(BF16))
        return m_new, l, acc

    m, l, acc = lax.fori_loop(0, i, body, (m, l, acc))
    o_ref[...] = acc / l


def _prompt_attn(qT, kmean, kx, vT):
    b, _, s = qT.shape
    nb = s // MOBA_BLOCK
    return pl.pallas_call(
        _prompt_attn_kernel,
        grid=(b, N_HEADS, nb),
        in_specs=[
            pl.BlockSpec((None, HEAD_DIM, MOBA_BLOCK), lambda bi, h, i: (bi, h, i)),
            pl.BlockSpec((None, None, nb, HEAD_DIM), lambda bi, h, i: (bi, h, 0, 0)),
            pl.BlockSpec((None, None, nb, MOBA_BLOCK, LANES), lambda bi, h, i: (bi, h, 0, 0, 0)),
            pl.BlockSpec((None, None, nb, HEAD_DIM, MOBA_BLOCK), lambda bi, h, i: (bi, h, 0, 0, 0)),
        ],
        out_specs=pl.BlockSpec((None, HEAD_DIM, MOBA_BLOCK), lambda bi, h, i: (bi, h, i)),
        out_shape=jax.ShapeDtypeStruct((b, ATT_WIDTH, s), F32),
        compiler_params=pltpu.CompilerParams(
            dimension_semantics=("arbitrary", "arbitrary", "arbitrary"), vmem_limit_bytes=VMEM_LIMIT),
        name="prompt_attn",
    )(qT, kmean, kx, vT)


def _pool_mix(diffs, wmix_ref, scale):
    mixed = [_dot(d.astype(BF16), wmix_ref[g]) for g, d in enumerate(diffs)]
    return jnp.concatenate(mixed, axis=1) * scale


def _layer_tail(x, att, zb, pool, za, ga, gb, p, wpo_ref, wao_ref, wo_ref, g_ref, b_ref, wple_ref, wgate_ref):
    y_pool = _dot((pool * _silu(za)).astype(BF16), wpo_ref[...])
    y_att = _dot((att * _silu(zb)).astype(BF16), wao_ref[...])
    merged = _sigmoid(ga) * y_pool + _sigmoid(gb) * y_att
    r = DEEPNORM_ALPHA * x + _dot(merged.astype(BF16), wo_ref[...])
    mu = jnp.mean(r, axis=-1, keepdims=True)
    c = r - mu
    var = jnp.mean(c * c, axis=-1, keepdims=True)
    h = c * lax.rsqrt(var + LN_EPS) * g_ref[...] + b_ref[...]
    gate = _sigmoid(_dot(h.astype(BF16), wgate_ref[...]))
    return h + gate * _dot(p.astype(BF16), wple_ref[...])


def _prompt_tail_kernel(x_ref, attT_ref, zb_ref, u_ref, uprev_ref, za_ref, ga_ref, gb_ref, p_ref,
                        wmix_ref, scale_ref, wpo_ref, wao_ref, wo_ref, g_ref, b_ref, wple_ref, wgate_ref,
                        y_ref):
    i = pl.program_id(1)
    u = u_ref[...]
    hist = jnp.where(i > 0, uprev_ref[...], 0.0)
    ext = jnp.concatenate([hist, u], axis=0)
    pos = i * ROW_TILE + lax.broadcasted_iota(jnp.int32, (ROW_TILE, POOL_GROUP), 0)
    diffs = []
    for g, w in enumerate(POOL_WINDOWS):
        e = ext[:, g * POOL_GROUP:(g + 1) * POOL_GROUP]
        step = 1
        while step < w:
            e = e + pltpu.roll(e, step, 0)
            step *= 2
        cnt = jnp.minimum(pos + 1, w).astype(F32)
        diffs.append(e[HIST_ROWS:] / cnt - u[:, g * POOL_GROUP:(g + 1) * POOL_GROUP])
    pool = _pool_mix(diffs, wmix_ref, scale_ref[...])
    y_ref[...] = _layer_tail(x_ref[...], attT_ref[...].T, zb_ref[...], pool, za_ref[...], ga_ref[...],
                             gb_ref[...], p_ref[...], wpo_ref, wao_ref, wo_ref, g_ref, b_ref,
                             wple_ref, wgate_ref)


def _full(shape):
    return pl.BlockSpec(shape, lambda *_: (0,) * len(shape))


def _tail_weight_specs():
    return [
        _full((len(POOL_WINDOWS), POOL_GROUP, POOL_GROUP)), _full((1, POOL_WIDTH)),
        _full((POOL_WIDTH, D_MODEL)), _full((ATT_WIDTH, D_MODEL)), _full((D_MODEL, D_MODEL)),
        _full((1, D_MODEL)), _full((1, D_MODEL)), _full((PLE_DIM, D_MODEL)), _full((D_MODEL, D_MODEL)),
    ]


def _prompt_tail(x, attT, zb, u, za, ga, gb, p, weights):
    b, s, _ = x.shape
    row = lambda width: pl.BlockSpec((None, ROW_TILE, width), lambda bi, i: (bi, i, 0))
    hist_blocks = ROW_TILE // HIST_ROWS
    in_specs = [
        row(D_MODEL),
        pl.BlockSpec((None, ATT_WIDTH, ROW_TILE), lambda bi, i: (bi, 0, i)),
        row(ATT_WIDTH), row(POOL_WIDTH),
        pl.BlockSpec((None, HIST_ROWS, POOL_WIDTH), lambda bi, i: (bi, jnp.maximum(i * hist_blocks - 1, 0), 0)),
        row(POOL_WIDTH), row(D_MODEL), row(D_MODEL), row(PLE_DIM),
    ] + _tail_weight_specs()
    return pl.pallas_call(
        _prompt_tail_kernel,
        grid=(b, s // ROW_TILE),
        in_specs=in_specs,
        out_specs=row(D_MODEL),
        out_shape=jax.ShapeDtypeStruct((b, s, D_MODEL), F32),
        compiler_params=pltpu.CompilerParams(
            dimension_semantics=("arbitrary", "arbitrary"), vmem_limit_bytes=VMEM_LIMIT),
        name="prompt_tail",
    )(x, attT, zb, u, u, za, ga, gb, p, *weights)


def _sample_inproj_kernel(x_ref, w_ref, cos_ref, sin_ref,
                          q_ref, k_ref, v_ref, zb_ref, u_ref, za_ref, ga_ref, gb_ref):
    xb = x_ref[...].astype(BF16)
    cos = cos_ref[...]
    sin = sin_ref[...]
    q_ref[...] = _rope(_project(xb, w_ref, 0), cos, sin) * Q_SCALE
    k_ref[...] = _rope(_project(xb, w_ref, 1), cos, sin)
    v_ref[...] = _project(xb, w_ref, 2)
    zb_ref[...] = _project(xb, w_ref, 3)
    u_ref[...] = _project(xb, w_ref, 4)
    za_ref[...] = _project(xb, w_ref, 5)
    ga_ref[...] = _project(xb, w_ref, 6)
    gb_ref[...] = _project(xb, w_ref, 7)


def _sample_inproj(x, w_in, cos, sin):
    n = x.shape[0]
    widths = (ATT_WIDTH, ATT_WIDTH, ATT_WIDTH, ATT_WIDTH, POOL_WIDTH, POOL_WIDTH, D_MODEL, D_MODEL)
    return pl.pallas_call(
        _sample_inproj_kernel,
        out_shape=[jax.ShapeDtypeStruct((n, w), F32) for w in widths],
        compiler_params=pltpu.CompilerParams(vmem_limit_bytes=VMEM_LIMIT),
        name="sample_inproj",
    )(x, w_in, cos, sin)


def _pagesum_select_kernel(pt_ref, *refs):
    pages = refs[:PAGES_PER_STEP]
    q_ref, ids_ref, ksum_ref = refs[PAGES_PER_STEP:]
    s = pl.program_id(1)
    blocks_per_step = PAGES_PER_STEP // PAGES_PER_BLOCK
    rows = []
    for blk in range(blocks_per_step):
        acc = None
        for r in range(PAGES_PER_BLOCK):
            part = jnp.sum(pages[blk * PAGES_PER_BLOCK + r][...], axis=0, keepdims=True)
            acc = part if acc is None else acc + part
        rows.append(acc)
    ksum_ref[pl.ds(pl.multiple_of(s * blocks_per_step, blocks_per_step), blocks_per_step), :] = (
        jnp.concatenate(rows, axis=0))

    @pl.when(s == pl.num_programs(1) - 1)
    def _():
        n_blk = ksum_ref.shape[0]
        prod = (ksum_ref[...] / MOBA_BLOCK) * q_ref[...]
        lane_head = lax.broadcasted_iota(jnp.int32, (ATT_WIDTH, LANES), 0) // HEAD_DIM
        col = lax.broadcasted_iota(jnp.int32, (ATT_WIDTH, LANES), 1)
        seg = jnp.where(lane_head == col, 1.0, 0.0).astype(BF16)
        hi = prod.astype(BF16)
        lo = (prod - hi.astype(F32)).astype(BF16)
        scores = _dot(hi, seg) + _dot(lo, seg)
        blk = lax.broadcasted_iota(jnp.int32, scores.shape, 0)
        out_row = lax.broadcasted_iota(jnp.int32, ids_ref.shape, 0)
        ids = jnp.zeros(ids_ref.shape, jnp.int32)
        sc = scores
        for t in range(MOBA_TOPK):
            m = jnp.max(sc, axis=0, keepdims=True)
            first = jnp.min(jnp.where(sc == m, blk, n_blk), axis=0, keepdims=True)
            ids = jnp.where(out_row == t, first, ids)
            sc = jnp.where(blk == first, -jnp.inf, sc)
        ids_ref[...] = ids


def _pagesum_select(page_table, cache_k, q):
    db, n_pages = page_table.shape
    n_blk = n_pages // PAGES_PER_BLOCK
    steps = n_pages // PAGES_PER_STEP
    assert n_pages % PAGES_PER_STEP == 0 and PAGES_PER_STEP % (8 * PAGES_PER_BLOCK) == 0
    page_spec = lambda r: pl.BlockSpec((None, PAGE_SIZE, ATT_WIDTH),
                                       lambda b, s, pt: (pt[b, s * PAGES_PER_STEP + r], 0, 0))
    grid_spec = pltpu.PrefetchScalarGridSpec(
        num_scalar_prefetch=1,
        grid=(db, steps),
        in_specs=[page_spec(r) for r in range(PAGES_PER_STEP)]
        + [pl.BlockSpec((None, 1, ATT_WIDTH), lambda b, s, pt: (b, 0, 0))],
        out_specs=pl.BlockSpec((None, 8, LANES), lambda b, s, pt: (b, 0, 0)),
        scratch_shapes=[pltpu.VMEM((n_blk, ATT_WIDTH), F32)],
    )
    return pl.pallas_call(
        _pagesum_select_kernel,
        grid_spec=grid_spec,
        out_shape=jax.ShapeDtypeStruct((db, 8, LANES), jnp.int32),
        compiler_params=pltpu.CompilerParams(
            dimension_semantics=("arbitrary", "arbitrary"), vmem_limit_bytes=VMEM_LIMIT),
        name="sample_pagesum_select",
    )(page_table, *([cache_k] * PAGES_PER_STEP), q)


def _sample_attn_kernel(pt_ref, ids_ref, *refs):
    n_sel = 2 * MOBA_TOPK * PAGES_PER_BLOCK
    k_pages = refs[:n_sel]
    v_pages = refs[n_sel:2 * n_sel]
    q_ref, kn_ref, vn_ref, o_ref = refs[2 * n_sel:]
    lane = lax.broadcasted_iota(jnp.int32, (1, LANES), 1)
    q2 = q_ref[...]
    kn = kn_ref[...]
    vn = vn_ref[...]
    per_head = MOBA_TOPK * PAGES_PER_BLOCK
    out = jnp.zeros((1, LANES), F32)
    for hh in range(2):
        own = (lane // HEAD_DIM) == hh
        qh = jnp.where(own, q2, 0.0)
        qh8 = jnp.broadcast_to(qh, (8, LANES)).astype(BF16)
        logits = [lax.dot_general(qh8, k_pages[hh * per_head + t][...].astype(BF16),
                                  (((1,), (1,)), ((), ())), preferred_element_type=F32)
                  for t in range(per_head)]
        s_new = jnp.sum(qh * kn, axis=1, keepdims=True)
        m = s_new
        for lg in logits:
            m = jnp.maximum(m, jnp.max(lg, axis=1, keepdims=True)[0:1])
        p_new = jnp.exp(s_new - m)
        l = p_new
        acc = p_new * vn
        for t, lg in enumerate(logits):
            p = jnp.exp(lg - m)
            l = l + jnp.sum(p, axis=1, keepdims=True)[0:1]
            acc = acc + _dot(p.astype(BF16), v_pages[hh * per_head + t][...].astype(BF16))[0:1]
        out = jnp.where(own, acc / l, out)
    o_ref[...] = out


def _sample_attn(page_table, ids, cache_k, cache_v, q, k_new, v_new):
    db = page_table.shape[0]
    pairs = N_HEADS // 2

    def page_spec(hh, t, r):
        def index(b, pr, pt, ids):
            blk = ids[(b * N_HEADS + pr * 2 + hh) * MOBA_TOPK + t]
            return (pt[b, blk * PAGES_PER_BLOCK + r], 0, pr)
        return pl.BlockSpec((None, PAGE_SIZE, LANES), index)

    page_specs = [page_spec(hh, t, r) for hh in range(2) for t in range(MOBA_TOPK)
                  for r in range(PAGES_PER_BLOCK)]
    vec_spec = pl.BlockSpec((None, 1, LANES), lambda b, pr, pt, ids: (b, 0, pr))
    grid_spec = pltpu.PrefetchScalarGridSpec(
        num_scalar_prefetch=2,
        grid=(db, pairs),
        in_specs=page_specs + page_specs + [vec_spec, vec_spec, vec_spec],
        out_specs=vec_spec,
    )
    n = len(page_specs)
    return pl.pallas_call(
        _sample_attn_kernel,
        grid_spec=grid_spec,
        out_shape=jax.ShapeDtypeStruct((db, 1, ATT_WIDTH), F32),
        compiler_params=pltpu.CompilerParams(
            dimension_semantics=("arbitrary", "arbitrary"), vmem_limit_bytes=VMEM_LIMIT),
        name="sample_attn",
    )(page_table, ids, *([cache_k] * n), *([cache_v] * n), q, k_new, v_new)


def _sample_tail_kernel(x_ref, att_ref, zb_ref, u_ref, state_ref, za_ref, ga_ref, gb_ref, p_ref,
                        wmix_ref, scale_ref, wpo_ref, wao_ref, wo_ref, g_ref, b_ref, wple_ref, wgate_ref,
                        y_ref):
    u = u_ref[...]
    diffs = []
    for g, w in enumerate(POOL_WINDOWS):
        lanes = slice(g * POOL_GROUP, (g + 1) * POOL_GROUP)
        total = u[:, lanes]
        for back in range(1, w):
            total = total + state_ref[POOL_STATE - back, :, lanes]
        diffs.append(total / float(w) - u[:, lanes])
    pool = _pool_mix(diffs, wmix_ref, scale_ref[...])
    y_ref[...] = _layer_tail(x_ref[...], att_ref[...], zb_ref[...], pool, za_ref[...], ga_ref[...],
                             gb_ref[...], p_ref[...], wpo_ref, wao_ref, wo_ref, g_ref, b_ref,
                             wple_ref, wgate_ref)


def _sample_tail(x, att, zb, u, state_t, za, ga, gb, p, weights):
    return pl.pallas_call(
        _sample_tail_kernel,
        out_shape=jax.ShapeDtypeStruct(x.shape, F32),
        compiler_params=pltpu.CompilerParams(vmem_limit_bytes=VMEM_LIMIT),
        name="sample_tail",
    )(x, att, zb, u, state_t, za, ga, gb, p, *weights)


def _rope_tables(pos):
    inv = ROPE_THETA ** (-jnp.arange(HALF_DIM, dtype=F32) / HALF_DIM)
    ang = pos.astype(F32)[:, None] * inv[None, :]
    cos = jnp.cos(ang)
    sin = jnp.sin(ang)
    return jnp.tile(cos, (1, 4)), jnp.tile(jnp.concatenate([-sin, sin], axis=1), (1, 2))


def kernel(x_prompt, x_sample, cache_k, cache_v, state_pool, page_table, p_prompt, p_sample, w_in, w_pool_mix, pool_scale, w_pool_out, w_att_out, w_o, ln_g, ln_b, w_ple, w_ple_gate):
    assert w_in.shape[0] == DEPTH
    b, s, _ = x_prompt.shape
    db, n, _ = x_sample.shape
    assert n == 1
    n_pages = page_table.shape[1]
    past_len = n_pages * PAGE_SIZE
    assert past_len % MOBA_BLOCK == 0 and past_len // MOBA_BLOCK >= MOBA_TOPK
    nb = s // MOBA_BLOCK

    w_in_b = w_in[0].astype(BF16)
    weights = (w_pool_mix[0].astype(BF16), pool_scale[0][None, :], w_pool_out[0].astype(BF16),
               w_att_out[0].astype(BF16), w_o[0].astype(BF16), ln_g[0][None, :], ln_b[0][None, :],
               w_ple[0].astype(BF16), w_ple_gate[0].astype(BF16))

    cos_p, sin_p = _rope_tables(jnp.arange(s))
    qT, kx, vT, k_p, v_p, kmean, zb, u_p, za, ga, gb = _prompt_inproj(x_prompt, w_in_b, cos_p, sin_p)
    kmean_h = kmean.reshape(b, nb, N_HEADS, HEAD_DIM).transpose(0, 2, 1, 3)
    attT = _prompt_attn(qT, kmean_h, kx, vT)
    y_prompt = _prompt_tail(x_prompt, attT, zb, u_p, za, ga, gb, p_prompt[0], weights)

    cos_s, sin_s = _rope_tables(jnp.full((1,), past_len))
    q_s, k_s, v_s, zb_s, u_s, za_s, ga_s, gb_s = _sample_inproj(x_sample[:, 0], w_in_b, cos_s, sin_s)
    n_phys = cache_k.shape[1]
    ck = cache_k[0].reshape(n_phys, PAGE_SIZE, ATT_WIDTH)
    cv = cache_v[0].reshape(n_phys, PAGE_SIZE, ATT_WIDTH)
    ids = _pagesum_select(page_table, ck, q_s[:, None, :])
    ids_flat = ids[:, :MOBA_TOPK, :N_HEADS].transpose(0, 2, 1).reshape(-1)
    att_s = _sample_attn(page_table, ids_flat, ck, cv, q_s[:, None, :], k_s[:, None, :], v_s[:, None, :])
    state = state_pool[0]
    y_sample = _sample_tail(x_sample[:, 0], att_s[:, 0], zb_s, u_s, state.transpose(1, 0, 2),
                            za_s, ga_s, gb_s, p_sample[0][:, 0], weights)

    heads = (N_HEADS, HEAD_DIM)
    pool_prompt = u_p[:, s - POOL_STATE:]
    pool_sample = jnp.concatenate([state[:, 1:], u_s[:, None, :]], axis=1)
    return (y_prompt, y_sample[:, None, :],
            k_p.reshape(1, b, s, *heads), v_p.reshape(1, b, s, *heads), pool_prompt[None],
            k_s.reshape(1, db, 1, *heads), v_s.reshape(1, db, 1, *heads), pool_sample[None])
```

```python
import functools

import numpy as np
import jax
import jax.numpy as jnp
from jax import lax
from jax.experimental import pallas as pl
from jax.experimental.pallas import tpu as pltpu

D_MODEL = 1024
N_HEADS = 8
HEAD_DIM = 64
HALF_DIM = HEAD_DIM // 2
ATT_WIDTH = N_HEADS * HEAD_DIM
POOL_WINDOWS = (2, 4, 8, 16)
POOL_WIDTH = D_MODEL // 2
POOL_GROUP = POOL_WIDTH // len(POOL_WINDOWS)
POOL_STATE = max(POOL_WINDOWS) - 1
MOBA_BLOCK = 256
MOBA_TOPK = 3
PAGE_SIZE = 128
PAGES_PER_BLOCK = MOBA_BLOCK // PAGE_SIZE
ROPE_THETA = 10000.0
PLE_DIM = 256
LN_EPS = 1e-5
DEPTH = 1
DEEPNORM_ALPHA = (2 * DEPTH) ** 0.25
IN_SPLITS = (ATT_WIDTH, ATT_WIDTH, ATT_WIDTH, ATT_WIDTH, POOL_WIDTH, POOL_WIDTH, D_MODEL, D_MODEL)
IN_OFFS = tuple(int(v) for v in np.cumsum((0,) + IN_SPLITS))
IN_WIDTH = IN_OFFS[-1]
Q_SCALE = HEAD_DIM ** -0.5 * float(np.log2(np.e))
SUM_ROWS = 16

LANES = 128
HIST_ROWS = 16
MASK_NEG = -1e30
ROW_TILE = 256
PAGES_PER_STEP = 16
VMEM_LIMIT = 52 * 1024 * 1024

F32 = jnp.float32
BF16 = jnp.bfloat16


def _sigmoid(x):
    return 1.0 / (1.0 + jnp.exp(-x))


def _silu(x):
    return x * _sigmoid(x)


def _dot(a, b):
    return jnp.dot(a, b, preferred_element_type=F32)


def _rope(t, cos, sin_signed):
    lane = lax.broadcasted_iota(jnp.int32, (1, LANES), 1)
    first_half = (lane % HEAD_DIM) < HALF_DIM
    outs = []
    for c in range(t.shape[1] // LANES):
        tc = t[:, c * LANES:(c + 1) * LANES]
        partner = jnp.where(first_half, pltpu.roll(tc, LANES - HALF_DIM, 1), pltpu.roll(tc, HALF_DIM, 1))
        outs.append(tc * cos + partner * sin_signed)
    return jnp.concatenate(outs, axis=1)


def _project(xb, w_ref, seg):
    return _dot(xb, w_ref[:, IN_OFFS[seg]:IN_OFFS[seg + 1]])


def _prompt_inproj_kernel(x_ref, w_ref, cos_ref, sin_ref,
                          qT_ref, kx_ref, vT_ref, k_ref, v_ref, km_ref,
                          zb_ref, u_ref, za_ref, ga_ref, gb_ref):
    i = pl.program_id(1)
    xb = x_ref[...].astype(BF16)
    cos = cos_ref[...]
    sin = sin_ref[...]

    q = _rope(_project(xb, w_ref, 0), cos, sin) * Q_SCALE
    qT_ref[...] = q.T.astype(BF16)

    k = _rope(_project(xb, w_ref, 1), cos, sin)
    k_ref[...] = k
    km_ref[...] = jnp.sum(k, axis=0, keepdims=True) / MOBA_BLOCK
    lane = lax.broadcasted_iota(jnp.int32, (ROW_TILE, LANES), 1)
    extra = jnp.where(lane - HEAD_DIM == i, 1.0, 0.0)
    for h in range(N_HEADS):
        pair = k[:, (h // 2) * LANES:(h // 2 + 1) * LANES]
        base = pair if h % 2 == 0 else pltpu.roll(pair, HEAD_DIM, 1)
        kx_ref[h, 0] = jnp.where(lane < HEAD_DIM, base, extra).astype(BF16)

    v = _project(xb, w_ref, 2)
    v_ref[...] = v
    vT = v.T.astype(BF16).reshape(N_HEADS, HEAD_DIM, ROW_TILE)
    vT_ref[:, 0] = jnp.concatenate([vT, jnp.ones((N_HEADS, SUM_ROWS, ROW_TILE), BF16)], axis=1)

    zb_ref[...] = _project(xb, w_ref, 3)
    u_ref[...] = _project(xb, w_ref, 4)
    za_ref[...] = _project(xb, w_ref, 5)
    ga_ref[...] = _project(xb, w_ref, 6)
    gb_ref[...] = _project(xb, w_ref, 7)


def _prompt_inproj(x, w_in, cos, sin):
    b, s, _ = x.shape
    nb = s // MOBA_BLOCK
    assert ROW_TILE == MOBA_BLOCK and s % ROW_TILE == 0 and nb <= 32
    row = lambda width: pl.BlockSpec((None, ROW_TILE, width), lambda bi, i: (bi, i, 0))
    out_shape = [
        jax.ShapeDtypeStruct((b, ATT_WIDTH, s), BF16),
        jax.ShapeDtypeStruct((b, N_HEADS, nb, MOBA_BLOCK, LANES), BF16),
        jax.ShapeDtypeStruct((b, N_HEADS, nb, HEAD_DIM + SUM_ROWS, MOBA_BLOCK), BF16),
        jax.ShapeDtypeStruct((b, s, ATT_WIDTH), F32),
        jax.ShapeDtypeStruct((b, s, ATT_WIDTH), F32),
        jax.ShapeDtypeStruct((b, nb, 1, ATT_WIDTH), F32),
        jax.ShapeDtypeStruct((b, s, ATT_WIDTH), F32),
        jax.ShapeDtypeStruct((b, s, POOL_WIDTH), F32),
        jax.ShapeDtypeStruct((b, s, POOL_WIDTH), F32),
        jax.ShapeDtypeStruct((b, s, D_MODEL), F32),
        jax.ShapeDtypeStruct((b, s, D_MODEL), F32),
    ]
    out_specs = [
        pl.BlockSpec((None, ATT_WIDTH, ROW_TILE), lambda bi, i: (bi, 0, i)),
        pl.BlockSpec((None, N_HEADS, 1, MOBA_BLOCK, LANES), lambda bi, i: (bi, 0, i, 0, 0)),
        pl.BlockSpec((None, N_HEADS, 1, HEAD_DIM + SUM_ROWS, MOBA_BLOCK), lambda bi, i: (bi, 0, i, 0, 0)),
        row(ATT_WIDTH), row(ATT_WIDTH),
        pl.BlockSpec((None, None, 1, ATT_WIDTH), lambda bi, i: (bi, i, 0, 0)),
        row(ATT_WIDTH), row(POOL_WIDTH), row(POOL_WIDTH), row(D_MODEL), row(D_MODEL),
    ]
    in_specs = [
        row(D_MODEL),
        pl.BlockSpec((D_MODEL, IN_WIDTH), lambda bi, i: (0, 0)),
        pl.BlockSpec((ROW_TILE, LANES), lambda bi, i: (i, 0)),
        pl.BlockSpec((ROW_TILE, LANES), lambda bi, i: (i, 0)),
    ]
    return pl.pallas_call(
        _prompt_inproj_kernel,
        grid=(b, s // ROW_TILE),
        in_specs=in_specs, out_specs=out_specs, out_shape=out_shape,
        compiler_params=pltpu.CompilerParams(
            dimension_semantics=("arbitrary", "arbitrary"), vmem_limit_bytes=VMEM_LIMIT),
        name="prompt_inproj",
    )(x, w_in, cos, sin)


def _select_bias(scores, n_past, own):
    blk = lax.broadcasted_iota(jnp.int32, scores.shape, 0)
    n_blk = scores.shape[0]
    s = jnp.where(blk < n_past, scores, -jnp.inf)
    chosen = blk == own
    for _ in range(MOBA_TOPK):
        m = jnp.max(s, axis=0, keepdims=True)
        cand = jnp.where((s == m) & (m > -jnp.inf), blk, n_blk)
        first = jnp.min(cand, axis=0, keepdims=True)
        pick = blk == first
        chosen = chosen | pick
        s = jnp.where(pick, -jnp.inf, s)
    return jnp.where(chosen, 0.0, MASK_NEG)


def _prompt_attn_kernel(qT_ref, km_ref, kx_ref, vT_ref, o_ref,
                        qx_scr, s_scr, p_scr, acc_scr, m_scr, alpha_scr):
    i = pl.program_id(1)
    n_blk = km_ref.shape[1]
    heads = range(N_HEADS)

    pad = jnp.zeros((LANES - HEAD_DIM - n_blk, MOBA_BLOCK), BF16)
    for h in heads:
        qT = qT_ref[h * HEAD_DIM:(h + 1) * HEAD_DIM, :]
        bias = _select_bias(_dot(km_ref[h].astype(BF16), qT), i, i)
        qx_scr[h] = jnp.concatenate([qT, bias.astype(BF16), pad], axis=0)
        m_scr[h] = jnp.full((1, MOBA_BLOCK), MASK_NEG, F32)
        acc_scr[h] = jnp.zeros((HEAD_DIM + SUM_ROWS, MOBA_BLOCK), F32)

    def scores(h, blk):
        s_scr[h] = _dot(kx_ref[h, blk], qx_scr[h])

    def softmax(h, keep=None):
        for c in range(MOBA_BLOCK // LANES):
            lanes = slice(c * LANES, (c + 1) * LANES)
            s = s_scr[h, :, lanes]
            if keep is not None:
                s = jnp.where(keep[:, lanes], s, MASK_NEG)
            m_old = m_scr[h, :, lanes]
            m_new = jnp.maximum(m_old, jnp.max(s, axis=0, keepdims=True))
            m_scr[h, :, lanes] = m_new
            alpha_scr[h, :, lanes] = jnp.exp2(m_old - m_new)
            p_scr[h, :, lanes] = jnp.exp2(s - m_new).astype(BF16)

    def accumulate(h, blk):
        acc_scr[h] = alpha_scr[h] * acc_scr[h] + _dot(vT_ref[h, blk], p_scr[h])

    key_row = lax.broadcasted_iota(jnp.int32, (MOBA_BLOCK, MOBA_BLOCK), 0)
    qry_col = lax.broadcasted_iota(jnp.int32, (MOBA_BLOCK, MOBA_BLOCK), 1)
    causal = key_row <= qry_col
    for h in heads:
        scores(h, i)
    for h in heads:
        softmax(h, causal)
    for h in heads:
        scores(h, 0)

    def body(t, _):
        blk_prev = jnp.where(t == 1, i, t - 2)
        for h in heads:
            accumulate(h, blk_prev)
        for h in heads:
            softmax(h)
        for h in heads:
            scores(h, t)
        return 0

    lax.fori_loop(1, i + 1, body, 0)
    blk_last = jnp.maximum(i - 1, 0)
    for h in heads:
        accumulate(h, blk_last)
        acc = acc_scr[h]
        o_ref[h * HEAD_DIM:(h + 1) * HEAD_DIM, :] = acc[:HEAD_DIM] / acc[HEAD_DIM:HEAD_DIM + 1]


def _prompt_attn(qT, kmean, kx, vT):
    b, _, s = qT.shape
    nb = s // MOBA_BLOCK
    once = pl.Buffered(1)
    return pl.pallas_call(
        _prompt_attn_kernel,
        grid=(b, nb),
        in_specs=[
            pl.BlockSpec((None, ATT_WIDTH, MOBA_BLOCK), lambda bi, i: (bi, 0, i)),
            pl.BlockSpec((None, N_HEADS, nb, HEAD_DIM), lambda bi, i: (bi, 0, 0, 0)),
            pl.BlockSpec((None, N_HEADS, nb, MOBA_BLOCK, LANES), lambda bi, i: (bi, 0, 0, 0, 0),
                         pipeline_mode=once),
            pl.BlockSpec((None, N_HEADS, nb, HEAD_DIM + SUM_ROWS, MOBA_BLOCK), lambda bi, i: (bi, 0, 0, 0, 0),
                         pipeline_mode=once),
        ],
        out_specs=pl.BlockSpec((None, ATT_WIDTH, MOBA_BLOCK), lambda bi, i: (bi, 0, i)),
        out_shape=jax.ShapeDtypeStruct((b, ATT_WIDTH, s), F32),
        scratch_shapes=[
            pltpu.VMEM((N_HEADS, LANES, MOBA_BLOCK), BF16),
            pltpu.VMEM((N_HEADS, MOBA_BLOCK, MOBA_BLOCK), F32),
            pltpu.VMEM((N_HEADS, MOBA_BLOCK, MOBA_BLOCK), BF16),
            pltpu.VMEM((N_HEADS, HEAD_DIM + SUM_ROWS, MOBA_BLOCK), F32),
            pltpu.VMEM((N_HEADS, 1, MOBA_BLOCK), F32),
            pltpu.VMEM((N_HEADS, 1, MOBA_BLOCK), F32),
        ],
        compiler_params=pltpu.CompilerParams(
            dimension_semantics=("arbitrary", "arbitrary"), vmem_limit_bytes=VMEM_LIMIT),
        name="prompt_attn",
    )(qT, kmean, kx, vT)


def _pool_mix(diffs, wmix_ref, scale):
    mixed = [_dot(d.astype(BF16), wmix_ref[g]) for g, d in enumerate(diffs)]
    return jnp.concatenate(mixed, axis=1) * scale


def _layer_tail(x, att, zb, pool, za, ga, gb, p, wpo_ref, wao_ref, wo_ref, g_ref, b_ref, wple_ref, wgate_ref):
    y_pool = _dot((pool * _silu(za)).astype(BF16), wpo_ref[...])
    y_att = _dot((att * _silu(zb)).astype(BF16), wao_ref[...])
    merged = _sigmoid(ga) * y_pool + _sigmoid(gb) * y_att
    r = DEEPNORM_ALPHA * x + _dot(merged.astype(BF16), wo_ref[...])
    mu = jnp.mean(r, axis=-1, keepdims=True)
    c = r - mu
    var = jnp.mean(c * c, axis=-1, keepdims=True)
    h = c * lax.rsqrt(var + LN_EPS) * g_ref[...] + b_ref[...]
    gate = _sigmoid(_dot(h.astype(BF16), wgate_ref[...]))
    return h + gate * _dot(p.astype(BF16), wple_ref[...])


def _prompt_tail_kernel(x_ref, attT_ref, zb_ref, u_ref, uprev_ref, za_ref, ga_ref, gb_ref, p_ref,
                        wmix_ref, scale_ref, wpo_ref, wao_ref, wo_ref, g_ref, b_ref, wple_ref, wgate_ref,
                        y_ref):
    i = pl.program_id(1)
    u = u_ref[...]
    hist = jnp.where(i > 0, uprev_ref[...], 0.0)
    ext = jnp.concatenate([hist, u], axis=0)
    pos = i * ROW_TILE + lax.broadcasted_iota(jnp.int32, (ROW_TILE, POOL_GROUP), 0)
    diffs = []
    for g, w in enumerate(POOL_WINDOWS):
        e = ext[:, g * POOL_GROUP:(g + 1) * POOL_GROUP]
        step = 1
        while step < w:
            e = e + pltpu.roll(e, step, 0)
            step *= 2
        cnt = jnp.minimum(pos + 1, w).astype(F32)
        diffs.append(e[HIST_ROWS:] / cnt - u[:, g * POOL_GROUP:(g + 1) * POOL_GROUP])
    pool = _pool_mix(diffs, wmix_ref, scale_ref[...])
    y_ref[...] = _layer_tail(x_ref[...], attT_ref[...].T, zb_ref[...], pool, za_ref[...], ga_ref[...],
                             gb_ref[...], p_ref[...], wpo_ref, wao_ref, wo_ref, g_ref, b_ref,
                             wple_ref, wgate_ref)


def _full(shape):
    return pl.BlockSpec(shape, lambda *_: (0,) * len(shape))


def _tail_weight_specs():
    return [
        _full((len(POOL_WINDOWS), POOL_GROUP, POOL_GROUP)), _full((1, POOL_WIDTH)),
        _full((POOL_WIDTH, D_MODEL)), _full((ATT_WIDTH, D_MODEL)), _full((D_MODEL, D_MODEL)),
        _full((1, D_MODEL)), _full((1, D_MODEL)), _full((PLE_DIM, D_MODEL)), _full((D_MODEL, D_MODEL)),
    ]


def _prompt_tail(x, attT, zb, u, za, ga, gb, p, weights):
    b, s, _ = x.shape
    row = lambda width: pl.BlockSpec((None, ROW_TILE, width), lambda bi, i: (bi, i, 0))
    hist_blocks = ROW_TILE // HIST_ROWS
    in_specs = [
        row(D_MODEL),
        pl.BlockSpec((None, ATT_WIDTH, ROW_TILE), lambda bi, i: (bi, 0, i)),
        row(ATT_WIDTH), row(POOL_WIDTH),
        pl.BlockSpec((None, HIST_ROWS, POOL_WIDTH), lambda bi, i: (bi, jnp.maximum(i * hist_blocks - 1, 0), 0)),
        row(POOL_WIDTH), row(D_MODEL), row(D_MODEL), row(PLE_DIM),
    ] + _tail_weight_specs()
    return pl.pallas_call(
        _prompt_tail_kernel,
        grid=(b, s // ROW_TILE),
        in_specs=in_specs,
        out_specs=row(D_MODEL),
        out_shape=jax.ShapeDtypeStruct((b, s, D_MODEL), F32),
        compiler_params=pltpu.CompilerParams(
            dimension_semantics=("arbitrary", "arbitrary"), vmem_limit_bytes=VMEM_LIMIT),
        name="prompt_tail",
    )(x, attT, zb, u, u, za, ga, gb, p, *weights)


def _sample_inproj_kernel(x_ref, w_ref, cos_ref, sin_ref,
                          q_ref, k_ref, v_ref, zb_ref, u_ref, za_ref, ga_ref, gb_ref):
    xb = x_ref[...].astype(BF16)
    cos = cos_ref[...]
    sin = sin_ref[...]
    q_ref[...] = _rope(_project(xb, w_ref, 0), cos, sin) * Q_SCALE
    k_ref[...] = _rope(_project(xb, w_ref, 1), cos, sin)
    v_ref[...] = _project(xb, w_ref, 2)
    zb_ref[...] = _project(xb, w_ref, 3)
    u_ref[...] = _project(xb, w_ref, 4)
    za_ref[...] = _project(xb, w_ref, 5)
    ga_ref[...] = _project(xb, w_ref, 6)
    gb_ref[...] = _project(xb, w_ref, 7)


def _sample_inproj(x, w_in, cos, sin):
    n = x.shape[0]
    widths = (ATT_WIDTH, ATT_WIDTH, ATT_WIDTH, ATT_WIDTH, POOL_WIDTH, POOL_WIDTH, D_MODEL, D_MODEL)
    return pl.pallas_call(
        _sample_inproj_kernel,
        out_shape=[jax.ShapeDtypeStruct((n, w), F32) for w in widths],
        compiler_params=pltpu.CompilerParams(vmem_limit_bytes=VMEM_LIMIT),
        name="sample_inproj",
    )(x, w_in, cos, sin)


def _pagesum_select_kernel(pt_ref, *refs):
    pages = refs[:PAGES_PER_STEP]
    q_ref, ids_ref, ksum_ref = refs[PAGES_PER_STEP:]
    s = pl.program_id(1)
    blocks_per_step = PAGES_PER_STEP // PAGES_PER_BLOCK
    for blk in range(blocks_per_step):
        acc = None
        for r in range(PAGES_PER_BLOCK):
            part = jnp.sum(pages[blk * PAGES_PER_BLOCK + r][...], axis=0)
            acc = part if acc is None else acc + part
        ksum_ref[s * blocks_per_step + blk] = acc

    @pl.when(s == pl.num_programs(1) - 1)
    def _():
        n_blk = ksum_ref.shape[0]
        kmean = ksum_ref[...] / MOBA_BLOCK
        sc = jnp.sum(kmean * q_ref[...][None], axis=2, keepdims=True)
        blk = lax.broadcasted_iota(jnp.int32, sc.shape, 0)
        for t in range(MOBA_TOPK):
            m = jnp.max(sc, axis=0, keepdims=True)
            first = jnp.min(jnp.where(sc == m, blk, n_blk), axis=0, keepdims=True)
            ids_ref[t] = first[0]
            sc = jnp.where(blk == first, -jnp.inf, sc)


def _pagesum_select(page_table, cache_k, q):
    db, n_pages = page_table.shape
    n_blk = n_pages // PAGES_PER_BLOCK
    steps = n_pages // PAGES_PER_STEP
    assert n_pages % PAGES_PER_STEP == 0 and PAGES_PER_STEP % PAGES_PER_BLOCK == 0
    page_spec = lambda r: pl.BlockSpec((None, None, PAGE_SIZE, N_HEADS, HEAD_DIM),
                                       lambda b, s, pt: (0, pt[b, s * PAGES_PER_STEP + r], 0, 0, 0))
    grid_spec = pltpu.PrefetchScalarGridSpec(
        num_scalar_prefetch=1,
        grid=(db, steps),
        in_specs=[page_spec(r) for r in range(PAGES_PER_STEP)]
        + [pl.BlockSpec((None, N_HEADS, HEAD_DIM), lambda b, s, pt: (b, 0, 0))],
        out_specs=pl.BlockSpec((None, MOBA_TOPK, N_HEADS, 1), lambda b, s, pt: (b, 0, 0, 0)),
        scratch_shapes=[pltpu.VMEM((n_blk, N_HEADS, HEAD_DIM), F32)],
    )
    return pl.pallas_call(
        _pagesum_select_kernel,
        grid_spec=grid_spec,
        out_shape=jax.ShapeDtypeStruct((db, MOBA_TOPK, N_HEADS, 1), jnp.int32),
        compiler_params=pltpu.CompilerParams(
            dimension_semantics=("arbitrary", "arbitrary"), vmem_limit_bytes=VMEM_LIMIT),
        name="sample_pagesum_select",
    )(page_table, *([cache_k] * PAGES_PER_STEP), q)


def _sample_attn_kernel(pt_ref, ids_ref, *refs):
    n_sel = MOBA_TOPK * PAGES_PER_BLOCK
    k_pages = refs[:n_sel]
    v_pages = refs[n_sel:2 * n_sel]
    q_ref, kn_ref, vn_ref, o_ref = refs[2 * n_sel:]
    h = pl.program_id(1)
    rows = PAGE_SIZE * N_HEADS
    own_head = lax.broadcasted_iota(jnp.int32, (8, rows), 1) % N_HEADS == h
    q = q_ref[...]
    q8 = jnp.broadcast_to(q, (8, HEAD_DIM)).astype(BF16)
    logits = []
    for kp in k_pages:
        flat = kp[...].reshape(rows, HEAD_DIM).astype(BF16)
        lg = lax.dot_general(q8, flat, (((1,), (1,)), ((), ())), preferred_element_type=F32)
        logits.append(jnp.where(own_head, lg, MASK_NEG))
    s_new = jnp.sum(q * kn_ref[...], axis=1, keepdims=True)
    m = s_new
    for lg in logits:
        m = jnp.maximum(m, jnp.max(lg, axis=1, keepdims=True)[0:1])
    p_new = jnp.exp2(s_new - m)
    l = p_new
    acc = p_new * vn_ref[...]
    for lg, vp in zip(logits, v_pages):
        p = jnp.exp2(lg - m)
        l = l + jnp.sum(p, axis=1, keepdims=True)[0:1]
        acc = acc + _dot(p.astype(BF16), vp[...].reshape(rows, HEAD_DIM).astype(BF16))[0:1]
    o_ref[...] = acc / l


def _sample_attn(page_table, ids, cache_k, cache_v, q, k_new, v_new):
    db = page_table.shape[0]

    def page_spec(t, r):
        def index(b, h, pt, ids):
            blk = ids[(b * N_HEADS + h) * MOBA_TOPK + t]
            return (0, pt[b, blk * PAGES_PER_BLOCK + r], 0, 0, 0)
        return pl.BlockSpec((None, None, PAGE_SIZE, N_HEADS, HEAD_DIM), index)

    def page_specs():
        return [page_spec(t, r) for t in range(MOBA_TOPK) for r in range(PAGES_PER_BLOCK)]

    vec_spec = lambda: pl.BlockSpec((None, None, 1, HEAD_DIM), lambda b, h, pt, ids: (b, h, 0, 0))
    grid_spec = pltpu.PrefetchScalarGridSpec(
        num_scalar_prefetch=2,
        grid=(db, N_HEADS),
        in_specs=page_specs() + page_specs() + [vec_spec(), vec_spec(), vec_spec()],
        out_specs=vec_spec(),
    )
    n = MOBA_TOPK * PAGES_PER_BLOCK
    return pl.pallas_call(
        _sample_attn_kernel,
        grid_spec=grid_spec,
        out_shape=jax.ShapeDtypeStruct((db, N_HEADS, 1, HEAD_DIM), F32),
        compiler_params=pltpu.CompilerParams(
            dimension_semantics=("arbitrary", "arbitrary"), vmem_limit_bytes=VMEM_LIMIT),
        name="sample_attn",
    )(page_table, ids, *([cache_k] * n), *([cache_v] * n), q, k_new, v_new)


def _sample_tail_kernel(x_ref, att_ref, zb_ref, u_ref, state_ref, za_ref, ga_ref, gb_ref, p_ref,
                        wmix_ref, scale_ref, wpo_ref, wao_ref, wo_ref, g_ref, b_ref, wple_ref, wgate_ref,
                        y_ref):
    u = u_ref[...]
    diffs = []
    for g, w in enumerate(POOL_WINDOWS):
        lanes = slice(g * POOL_GROUP, (g + 1) * POOL_GROUP)
        total = u[:, lanes]
        for back in range(1, w):
            total = total + state_ref[POOL_STATE - back, :, lanes]
        diffs.append(total / float(w) - u[:, lanes])
    pool = _pool_mix(diffs, wmix_ref, scale_ref[...])
    y_ref[...] = _layer_tail(x_ref[...], att_ref[...], zb_ref[...], pool, za_ref[...], ga_ref[...],
                             gb_ref[...], p_ref[...], wpo_ref, wao_ref, wo_ref, g_ref, b_ref,
                             wple_ref, wgate_ref)


def _sample_tail(x, att, zb, u, state_t, za, ga, gb, p, weights):
    return pl.pallas_call(
        _sample_tail_kernel,
        out_shape=jax.ShapeDtypeStruct(x.shape, F32),
        compiler_params=pltpu.CompilerParams(vmem_limit_bytes=VMEM_LIMIT),
        name="sample_tail",
    )(x, att, zb, u, state_t, za, ga, gb, p, *weights)


def _rope_tables(pos):
    inv = ROPE_THETA ** (-jnp.arange(HALF_DIM, dtype=F32) / HALF_DIM)
    ang = pos.astype(F32)[:, None] * inv[None, :]
    cos = jnp.cos(ang)
    sin = jnp.sin(ang)
    return jnp.tile(cos, (1, 4)), jnp.tile(jnp.concatenate([-sin, sin], axis=1), (1, 2))


def kernel(x_prompt, x_sample, cache_k, cache_v, state_pool, page_table, p_prompt, p_sample, w_in, w_pool_mix, pool_scale, w_pool_out, w_att_out, w_o, ln_g, ln_b, w_ple, w_ple_gate):
    assert w_in.shape[0] == DEPTH
    b, s, _ = x_prompt.shape
    db, n, _ = x_sample.shape
    assert n == 1
    n_pages = page_table.shape[1]
    past_len = n_pages * PAGE_SIZE
    assert past_len % MOBA_BLOCK == 0 and past_len // MOBA_BLOCK >= MOBA_TOPK
    nb = s // MOBA_BLOCK

    w_in_b = w_in[0].astype(BF16)
    weights = (w_pool_mix[0].astype(BF16), pool_scale[0][None, :], w_pool_out[0].astype(BF16),
               w_att_out[0].astype(BF16), w_o[0].astype(BF16), ln_g[0][None, :], ln_b[0][None, :],
               w_ple[0].astype(BF16), w_ple_gate[0].astype(BF16))

    cos_p, sin_p = _rope_tables(jnp.arange(s))
    qT, kx, vT, k_p, v_p, kmean, zb, u_p, za, ga, gb = _prompt_inproj(x_prompt, w_in_b, cos_p, sin_p)
    kmean_h = kmean.reshape(b, nb, N_HEADS, HEAD_DIM).transpose(0, 2, 1, 3)
    attT = _prompt_attn(qT, kmean_h, kx, vT)
    y_prompt = _prompt_tail(x_prompt, attT, zb, u_p, za, ga, gb, p_prompt[0], weights)

    cos_s, sin_s = _rope_tables(jnp.full((1,), past_len))
    q_s, k_s, v_s, zb_s, u_s, za_s, ga_s, gb_s = _sample_inproj(x_sample[:, 0], w_in_b, cos_s, sin_s)
    by_head = lambda t: t.reshape(db, N_HEADS, 1, HEAD_DIM)
    ids = _pagesum_select(page_table, cache_k, q_s.reshape(db, N_HEADS, HEAD_DIM))
    ids_flat = ids[..., 0].transpose(0, 2, 1).reshape(-1)
    att_s = _sample_attn(page_table, ids_flat, cache_k, cache_v, by_head(q_s), by_head(k_s), by_head(v_s))
    state = state_pool[0]
    y_sample = _sample_tail(x_sample[:, 0], att_s.reshape(db, ATT_WIDTH), zb_s, u_s, state.transpose(1, 0, 2),
                            za_s, ga_s, gb_s, p_sample[0][:, 0], weights)

    heads = (N_HEADS, HEAD_DIM)
    pool_prompt = u_p[:, s - POOL_STATE:]
    pool_sample = jnp.concatenate([state[:, 1:], u_s[:, None, :]], axis=1)
    return (y_prompt, y_sample[:, None, :],
            k_p.reshape(1, b, s, *heads), v_p.reshape(1, b, s, *heads), pool_prompt[None],
            k_s.reshape(1, db, 1, *heads), v_s.reshape(1, db, 1, *heads), pool_sample[None])
```

```python
import functools

import numpy as np
import jax
import jax.numpy as jnp
from jax import lax
from jax.experimental import pallas as pl
from jax.experimental.pallas import tpu as pltpu

D_MODEL = 1024
N_HEADS = 8
HEAD_DIM = 64
HALF_DIM = HEAD_DIM // 2
ATT_WIDTH = N_HEADS * HEAD_DIM
POOL_WINDOWS = (2, 4, 8, 16)
POOL_WIDTH = D_MODEL // 2
POOL_GROUP = POOL_WIDTH // len(POOL_WINDOWS)
POOL_STATE = max(POOL_WINDOWS) - 1
MOBA_BLOCK = 256
MOBA_TOPK = 3
PAGE_SIZE = 128
PAGES_PER_BLOCK = MOBA_BLOCK // PAGE_SIZE
ROPE_THETA = 10000.0
PLE_DIM = 256
LN_EPS = 1e-5
DEPTH = 1
DEEPNORM_ALPHA = (2 * DEPTH) ** 0.25
IN_SPLITS = (ATT_WIDTH, ATT_WIDTH, ATT_WIDTH, ATT_WIDTH, POOL_WIDTH, POOL_WIDTH, D_MODEL, D_MODEL)
IN_OFFS = tuple(int(v) for v in np.cumsum((0,) + IN_SPLITS))
IN_WIDTH = IN_OFFS[-1]
Q_SCALE = HEAD_DIM ** -0.5 * float(np.log2(np.e))
SUM_ROWS = 16

LANES = 128
HIST_ROWS = 16
MASK_NEG = -1e30
ROW_TILE = 256
PAGES_PER_STEP = 16
VMEM_LIMIT = 52 * 1024 * 1024

F32 = jnp.float32
BF16 = jnp.bfloat16


def _sigmoid(x):
    return 1.0 / (1.0 + jnp.exp(-x))


def _silu(x):
    return x * _sigmoid(x)


def _dot(a, b):
    return jnp.dot(a, b, preferred_element_type=F32)


def _rope(t, cos, sin_signed):
    lane = lax.broadcasted_iota(jnp.int32, (1, LANES), 1)
    first_half = (lane % HEAD_DIM) < HALF_DIM
    outs = []
    for c in range(t.shape[1] // LANES):
        tc = t[:, c * LANES:(c + 1) * LANES]
        partner = jnp.where(first_half, pltpu.roll(tc, LANES - HALF_DIM, 1), pltpu.roll(tc, HALF_DIM, 1))
        outs.append(tc * cos + partner * sin_signed)
    return jnp.concatenate(outs, axis=1)


def _project(xb, w_ref, seg):
    return _dot(xb, w_ref[:, IN_OFFS[seg]:IN_OFFS[seg + 1]])


def _prompt_inproj_kernel(x_ref, w_ref, cos_ref, sin_ref,
                          qT_ref, kx_ref, vT_ref, k_ref, v_ref, km_ref,
                          zb_ref, u_ref, za_ref, ga_ref, gb_ref):
    i = pl.program_id(1)
    xb = x_ref[...].astype(BF16)
    cos = cos_ref[...]
    sin = sin_ref[...]

    q = _rope(_project(xb, w_ref, 0), cos, sin) * Q_SCALE
    qT_ref[...] = q.T.astype(BF16)

    k = _rope(_project(xb, w_ref, 1), cos, sin)
    k_ref[...] = k
    km_ref[...] = jnp.sum(k, axis=0, keepdims=True) / MOBA_BLOCK
    lane = lax.broadcasted_iota(jnp.int32, (ROW_TILE, LANES), 1)
    extra = jnp.where(lane - HEAD_DIM == i, 1.0, 0.0)
    for h in range(N_HEADS):
        pair = k[:, (h // 2) * LANES:(h // 2 + 1) * LANES]
        base = pair if h % 2 == 0 else pltpu.roll(pair, HEAD_DIM, 1)
        kx_ref[h, 0] = jnp.where(lane < HEAD_DIM, base, extra).astype(BF16)

    v = _project(xb, w_ref, 2)
    v_ref[...] = v
    vT = v.T.astype(BF16).reshape(N_HEADS, HEAD_DIM, ROW_TILE)
    vT_ref[:, 0] = jnp.concatenate([vT, jnp.ones((N_HEADS, SUM_ROWS, ROW_TILE), BF16)], axis=1)

    zb_ref[...] = _project(xb, w_ref, 3)
    u_ref[...] = _project(xb, w_ref, 4)
    za_ref[...] = _project(xb, w_ref, 5)
    ga_ref[...] = _project(xb, w_ref, 6)
    gb_ref[...] = _project(xb, w_ref, 7)


def _prompt_inproj(x, w_in, cos, sin):
    b, s, _ = x.shape
    nb = s // MOBA_BLOCK
    assert ROW_TILE == MOBA_BLOCK and s % ROW_TILE == 0 and nb <= 32
    row = lambda width: pl.BlockSpec((None, ROW_TILE, width), lambda bi, i: (bi, i, 0))
    out_shape = [
        jax.ShapeDtypeStruct((b, ATT_WIDTH, s), BF16),
        jax.ShapeDtypeStruct((b, N_HEADS, nb, MOBA_BLOCK, LANES), BF16),
        jax.ShapeDtypeStruct((b, N_HEADS, nb, HEAD_DIM + SUM_ROWS, MOBA_BLOCK), BF16),
        jax.ShapeDtypeStruct((b, s, ATT_WIDTH), F32),
        jax.ShapeDtypeStruct((b, s, ATT_WIDTH), F32),
        jax.ShapeDtypeStruct((b, nb, 1, ATT_WIDTH), F32),
        jax.ShapeDtypeStruct((b, s, ATT_WIDTH), F32),
        jax.ShapeDtypeStruct((b, s, POOL_WIDTH), F32),
        jax.ShapeDtypeStruct((b, s, POOL_WIDTH), F32),
        jax.ShapeDtypeStruct((b, s, D_MODEL), F32),
        jax.ShapeDtypeStruct((b, s, D_MODEL), F32),
    ]
    out_specs = [
        pl.BlockSpec((None, ATT_WIDTH, ROW_TILE), lambda bi, i: (bi, 0, i)),
        pl.BlockSpec((None, N_HEADS, 1, MOBA_BLOCK, LANES), lambda bi, i: (bi, 0, i, 0, 0)),
        pl.BlockSpec((None, N_HEADS, 1, HEAD_DIM + SUM_ROWS, MOBA_BLOCK), lambda bi, i: (bi, 0, i, 0, 0)),
        row(ATT_WIDTH), row(ATT_WIDTH),
        pl.BlockSpec((None, None, 1, ATT_WIDTH), lambda bi, i: (bi, i, 0, 0)),
        row(ATT_WIDTH), row(POOL_WIDTH), row(POOL_WIDTH), row(D_MODEL), row(D_MODEL),
    ]
    in_specs = [
        row(D_MODEL),
        pl.BlockSpec((D_MODEL, IN_WIDTH), lambda bi, i: (0, 0)),
        pl.BlockSpec((ROW_TILE, LANES), lambda bi, i: (i, 0)),
        pl.BlockSpec((ROW_TILE, LANES), lambda bi, i: (i, 0)),
    ]
    return pl.pallas_call(
        _prompt_inproj_kernel,
        grid=(b, s // ROW_TILE),
        in_specs=in_specs, out_specs=out_specs, out_shape=out_shape,
        compiler_params=pltpu.CompilerParams(
            dimension_semantics=("arbitrary", "arbitrary"), vmem_limit_bytes=VMEM_LIMIT),
        name="prompt_inproj",
    )(x, w_in, cos, sin)


def _select_bias(scores, n_past, own):
    blk = lax.broadcasted_iota(jnp.int32, scores.shape, 0)
    n_blk = scores.shape[0]
    s = jnp.where(blk < n_past, scores, -jnp.inf)
    chosen = blk == own
    for _ in range(MOBA_TOPK):
        m = jnp.max(s, axis=0, keepdims=True)
        cand = jnp.where((s == m) & (m > -jnp.inf), blk, n_blk)
        first = jnp.min(cand, axis=0, keepdims=True)
        pick = blk == first
        chosen = chosen | pick
        s = jnp.where(pick, -jnp.inf, s)
    return jnp.where(chosen, 0.0, MASK_NEG)


def _prompt_attn_kernel(qT_ref, km_ref, kx_ref, vT_ref, o_ref,
                        qx_scr, s_scr, p_scr, acc_scr, m_scr, alpha_scr):
    i = pl.program_id(1)
    n_blk = km_ref.shape[1]
    heads = range(N_HEADS)

    pad = jnp.zeros((LANES - HEAD_DIM - n_blk, MOBA_BLOCK), BF16)
    for h in heads:
        qT = qT_ref[h * HEAD_DIM:(h + 1) * HEAD_DIM, :]
        bias = _select_bias(_dot(km_ref[h].astype(BF16), qT), i, i)
        qx_scr[h] = jnp.concatenate([qT, bias.astype(BF16), pad], axis=0)
        m_scr[h] = jnp.full((1, MOBA_BLOCK), MASK_NEG, F32)
        acc_scr[h] = jnp.zeros((HEAD_DIM + SUM_ROWS, MOBA_BLOCK), F32)

    def scores(h, blk):
        s_scr[h] = _dot(kx_ref[h, blk], qx_scr[h])

    def softmax(h, keep=None):
        for c in range(MOBA_BLOCK // LANES):
            lanes = slice(c * LANES, (c + 1) * LANES)
            s = s_scr[h, :, lanes]
            if keep is not None:
                s = jnp.where(keep[:, lanes], s, MASK_NEG)
            m_old = m_scr[h, :, lanes]
            m_new = jnp.maximum(m_old, jnp.max(s, axis=0, keepdims=True))
            m_scr[h, :, lanes] = m_new
            alpha_scr[h, :, lanes] = jnp.exp2(m_old - m_new)
            p_scr[h, :, lanes] = jnp.exp2(s - m_new).astype(BF16)

    def accumulate(h, blk):
        acc_scr[h] = alpha_scr[h] * acc_scr[h] + _dot(vT_ref[h, blk], p_scr[h])

    key_row = lax.broadcasted_iota(jnp.int32, (MOBA_BLOCK, MOBA_BLOCK), 0)
    qry_col = lax.broadcasted_iota(jnp.int32, (MOBA_BLOCK, MOBA_BLOCK), 1)
    causal = key_row <= qry_col
    for h in heads:
        scores(h, i)
    for h in heads:
        softmax(h, causal)
    for h in heads:
        scores(h, 0)

    def body(t, _):
        blk_prev = jnp.where(t == 1, i, t - 2)
        for h in heads:
            accumulate(h, blk_prev)
        for h in heads:
            softmax(h)
        for h in heads:
            scores(h, t)
        return 0

    lax.fori_loop(1, i + 1, body, 0)
    blk_last = jnp.maximum(i - 1, 0)
    for h in heads:
        accumulate(h, blk_last)
        acc = acc_scr[h]
        o_ref[h * HEAD_DIM:(h + 1) * HEAD_DIM, :] = acc[:HEAD_DIM] / acc[HEAD_DIM:HEAD_DIM + 1]


def _prompt_attn(qT, kmean, kx, vT):
    b, _, s = qT.shape
    nb = s // MOBA_BLOCK
    once = pl.Buffered(1)
    return pl.pallas_call(
        _prompt_attn_kernel,
        grid=(b, nb),
        in_specs=[
            pl.BlockSpec((None, ATT_WIDTH, MOBA_BLOCK), lambda bi, i: (bi, 0, i)),
            pl.BlockSpec((None, N_HEADS, nb, HEAD_DIM), lambda bi, i: (bi, 0, 0, 0)),
            pl.BlockSpec((None, N_HEADS, nb, MOBA_BLOCK, LANES), lambda bi, i: (bi, 0, 0, 0, 0),
                         pipeline_mode=once),
            pl.BlockSpec((None, N_HEADS, nb, HEAD_DIM + SUM_ROWS, MOBA_BLOCK), lambda bi, i: (bi, 0, 0, 0, 0),
                         pipeline_mode=once),
        ],
        out_specs=pl.BlockSpec((None, ATT_WIDTH, MOBA_BLOCK), lambda bi, i: (bi, 0, i)),
        out_shape=jax.ShapeDtypeStruct((b, ATT_WIDTH, s), F32),
        scratch_shapes=[
            pltpu.VMEM((N_HEADS, LANES, MOBA_BLOCK), BF16),
            pltpu.VMEM((N_HEADS, MOBA_BLOCK, MOBA_BLOCK), F32),
            pltpu.VMEM((N_HEADS, MOBA_BLOCK, MOBA_BLOCK), BF16),
            pltpu.VMEM((N_HEADS, HEAD_DIM + SUM_ROWS, MOBA_BLOCK), F32),
            pltpu.VMEM((N_HEADS, 1, MOBA_BLOCK), F32),
            pltpu.VMEM((N_HEADS, 1, MOBA_BLOCK), F32),
        ],
        compiler_params=pltpu.CompilerParams(
            dimension_semantics=("arbitrary", "arbitrary"), vmem_limit_bytes=VMEM_LIMIT),
        name="prompt_attn",
    )(qT, kmean, kx, vT)


def _pool_mix(diffs, wmix_ref, scale):
    mixed = [_dot(d.astype(BF16), wmix_ref[g]) for g, d in enumerate(diffs)]
    return jnp.concatenate(mixed, axis=1) * scale


def _layer_tail(x, att, zb, pool, za, ga, gb, p, wpo_ref, wao_ref, wo_ref, g_ref, b_ref, wple_ref, wgate_ref):
    y_pool = _dot((pool * _silu(za)).astype(BF16), wpo_ref[...])
    y_att = _dot((att * _silu(zb)).astype(BF16), wao_ref[...])
    merged = _sigmoid(ga) * y_pool + _sigmoid(gb) * y_att
    r = DEEPNORM_ALPHA * x + _dot(merged.astype(BF16), wo_ref[...])
    mu = jnp.mean(r, axis=-1, keepdims=True)
    c = r - mu
    var = jnp.mean(c * c, axis=-1, keepdims=True)
    h = c * lax.rsqrt(var + LN_EPS) * g_ref[...] + b_ref[...]
    gate = _sigmoid(_dot(h.astype(BF16), wgate_ref[...]))
    return h + gate * _dot(p.astype(BF16), wple_ref[...])


def _prompt_tail_kernel(x_ref, attT_ref, zb_ref, u_ref, uprev_ref, za_ref, ga_ref, gb_ref, p_ref,
                        wmix_ref, scale_ref, wpo_ref, wao_ref, wo_ref, g_ref, b_ref, wple_ref, wgate_ref,
                        y_ref):
    i = pl.program_id(1)
    u = u_ref[...]
    hist = jnp.where(i > 0, uprev_ref[...], 0.0)
    ext = jnp.concatenate([hist, u], axis=0)
    pos = i * ROW_TILE + lax.broadcasted_iota(jnp.int32, (ROW_TILE, POOL_GROUP), 0)
    diffs = []
    for g, w in enumerate(POOL_WINDOWS):
        e = ext[:, g * POOL_GROUP:(g + 1) * POOL_GROUP]
        step = 1
        while step < w:
            e = e + pltpu.roll(e, step, 0)
            step *= 2
        cnt = jnp.minimum(pos + 1, w).astype(F32)
        diffs.append(e[HIST_ROWS:] / cnt - u[:, g * POOL_GROUP:(g + 1) * POOL_GROUP])
    pool = _pool_mix(diffs, wmix_ref, scale_ref[...])
    y_ref[...] = _layer_tail(x_ref[...], attT_ref[...].T, zb_ref[...], pool, za_ref[...], ga_ref[...],
                             gb_ref[...], p_ref[...], wpo_ref, wao_ref, wo_ref, g_ref, b_ref,
                             wple_ref, wgate_ref)


def _full(shape):
    return pl.BlockSpec(shape, lambda *_: (0,) * len(shape))


def _tail_weight_specs():
    return [
        _full((len(POOL_WINDOWS), POOL_GROUP, POOL_GROUP)), _full((1, POOL_WIDTH)),
        _full((POOL_WIDTH, D_MODEL)), _full((ATT_WIDTH, D_MODEL)), _full((D_MODEL, D_MODEL)),
        _full((1, D_MODEL)), _full((1, D_MODEL)), _full((PLE_DIM, D_MODEL)), _full((D_MODEL, D_MODEL)),
    ]


def _prompt_tail(x, attT, zb, u, za, ga, gb, p, weights):
    b, s, _ = x.shape
    row = lambda width: pl.BlockSpec((None, ROW_TILE, width), lambda bi, i: (bi, i, 0))
    hist_blocks = ROW_TILE // HIST_ROWS
    in_specs = [
        row(D_MODEL),
        pl.BlockSpec((None, ATT_WIDTH, ROW_TILE), lambda bi, i: (bi, 0, i)),
        row(ATT_WIDTH), row(POOL_WIDTH),
        pl.BlockSpec((None, HIST_ROWS, POOL_WIDTH), lambda bi, i: (bi, jnp.maximum(i * hist_blocks - 1, 0), 0)),
        row(POOL_WIDTH), row(D_MODEL), row(D_MODEL), row(PLE_DIM),
    ] + _tail_weight_specs()
    return pl.pallas_call(
        _prompt_tail_kernel,
        grid=(b, s // ROW_TILE),
        in_specs=in_specs,
        out_specs=row(D_MODEL),
        out_shape=jax.ShapeDtypeStruct((b, s, D_MODEL), F32),
        compiler_params=pltpu.CompilerParams(
            dimension_semantics=("arbitrary", "arbitrary"), vmem_limit_bytes=VMEM_LIMIT),
        name="prompt_tail",
    )(x, attT, zb, u, u, za, ga, gb, p, *weights)


def _sample_inproj_kernel(x_ref, w_ref, cos_ref, sin_ref,
                          q_ref, k_ref, v_ref, zb_ref, u_ref, za_ref, ga_ref, gb_ref):
    xb = x_ref[...].astype(BF16)
    cos = cos_ref[...]
    sin = sin_ref[...]
    q_ref[...] = _rope(_project(xb, w_ref, 0), cos, sin) * Q_SCALE
    k_ref[...] = _rope(_project(xb, w_ref, 1), cos, sin)
    v_ref[...] = _project(xb, w_ref, 2)
    zb_ref[...] = _project(xb, w_ref, 3)
    u_ref[...] = _project(xb, w_ref, 4)
    za_ref[...] = _project(xb, w_ref, 5)
    ga_ref[...] = _project(xb, w_ref, 6)
    gb_ref[...] = _project(xb, w_ref, 7)


def _sample_inproj(x, w_in, cos, sin):
    n = x.shape[0]
    widths = (ATT_WIDTH, ATT_WIDTH, ATT_WIDTH, ATT_WIDTH, POOL_WIDTH, POOL_WIDTH, D_MODEL, D_MODEL)
    return pl.pallas_call(
        _sample_inproj_kernel,
        out_shape=[jax.ShapeDtypeStruct((n, w), F32) for w in widths],
        compiler_params=pltpu.CompilerParams(vmem_limit_bytes=VMEM_LIMIT),
        name="sample_inproj",
    )(x, w_in, cos, sin)


def _sample_probs_kernel(pt_ref, *refs):
    pages = refs[:PAGES_PER_STEP]
    q_ref, kn_ref, psel_ref, ids_ref, den_ref, pnew_ref, logit_scr = refs[PAGES_PER_STEP:]
    s = pl.program_id(1)
    blocks_per_step = PAGES_PER_STEP // PAGES_PER_BLOCK

    qb = q_ref[...]
    rows = []
    for blk in range(blocks_per_step):
        halves = [jnp.sum(pages[blk * PAGES_PER_BLOCK + r][...] * qb, axis=1, keepdims=True)
                  for r in range(PAGES_PER_BLOCK)]
        rows.append(jnp.concatenate(halves, axis=2))
    logit_scr[:, pl.ds(pl.multiple_of(s * blocks_per_step, blocks_per_step), blocks_per_step), :] = (
        jnp.concatenate(rows, axis=1))

    @pl.when(s == pl.num_programs(1) - 1)
    def _():
        logits = logit_scr[...]
        n_blk = logits.shape[1]
        sc = jnp.sum(logits, axis=2, keepdims=True) / MOBA_BLOCK
        blk = lax.broadcasted_iota(jnp.int32, sc.shape, 1)
        picks = []
        for t in range(MOBA_TOPK):
            m = jnp.max(sc, axis=1, keepdims=True)
            first = jnp.min(jnp.where(sc == m, blk, n_blk), axis=1, keepdims=True)
            ids_ref[:, t:t + 1, :] = jnp.broadcast_to(first, (N_HEADS, 1, LANES))
            picks.append(blk == first)
            sc = jnp.where(picks[-1], -jnp.inf, sc)
        chosen = picks[0]
        for pick in picks[1:]:
            chosen = chosen | pick
        s_new = jnp.sum(qb * kn_ref[...], axis=1, keepdims=True)[:, :, 0:1]
        masked = jnp.where(chosen, logits, MASK_NEG)
        m = jnp.max(jnp.max(masked, axis=2, keepdims=True), axis=1, keepdims=True)
        m = jnp.maximum(m, s_new)
        p = jnp.exp2(masked - m)
        p_new = jnp.exp2(s_new - m)
        den = jnp.sum(jnp.sum(p, axis=2, keepdims=True), axis=1, keepdims=True) + p_new
        den_ref[...] = jnp.broadcast_to(den, den_ref.shape)
        pnew_ref[...] = jnp.broadcast_to(p_new, pnew_ref.shape)
        for t, pick in enumerate(picks):
            psel_ref[:, t:t + 1, :] = jnp.sum(jnp.where(pick, p, 0.0), axis=1, keepdims=True)


def _sample_probs(page_table, cache_kT, q, k_new):
    db, n_pages = page_table.shape
    n_blk = n_pages // PAGES_PER_BLOCK
    steps = n_pages // PAGES_PER_STEP
    assert n_pages % PAGES_PER_STEP == 0 and PAGES_PER_STEP % (8 * PAGES_PER_BLOCK) == 0
    page_spec = lambda r: pl.BlockSpec((None, None, N_HEADS, HEAD_DIM, PAGE_SIZE),
                                       lambda b, s, pt: (0, pt[b, s * PAGES_PER_STEP + r], 0, 0, 0))
    per_seq = lambda *tail: pl.BlockSpec((None, N_HEADS) + tail, lambda b, s, pt: (b,) + (0,) * (len(tail) + 1))
    grid_spec = pltpu.PrefetchScalarGridSpec(
        num_scalar_prefetch=1,
        grid=(db, steps),
        in_specs=[page_spec(r) for r in range(PAGES_PER_STEP)]
        + [per_seq(HEAD_DIM, PAGE_SIZE), per_seq(HEAD_DIM, PAGE_SIZE)],
        out_specs=[per_seq(MOBA_TOPK, MOBA_BLOCK), per_seq(MOBA_TOPK, LANES), per_seq(1, LANES), per_seq(1, LANES)],
        scratch_shapes=[pltpu.VMEM((N_HEADS, n_blk, MOBA_BLOCK), F32)],
    )
    return pl.pallas_call(
        _sample_probs_kernel,
        grid_spec=grid_spec,
        out_shape=[jax.ShapeDtypeStruct((db, N_HEADS, MOBA_TOPK, MOBA_BLOCK), F32),
                   jax.ShapeDtypeStruct((db, N_HEADS, MOBA_TOPK, LANES), jnp.int32),
                   jax.ShapeDtypeStruct((db, N_HEADS, 1, LANES), F32),
                   jax.ShapeDtypeStruct((db, N_HEADS, 1, LANES), F32)],
        compiler_params=pltpu.CompilerParams(
            dimension_semantics=("arbitrary", "arbitrary"), vmem_limit_bytes=VMEM_LIMIT),
        name="sample_probs",
    )(page_table, *([cache_kT] * PAGES_PER_STEP), q, k_new)


def _sample_pv_kernel(pt_ref, ids_ref, *refs):
    n_slabs = MOBA_TOPK * PAGES_PER_BLOCK
    slabs = refs[:n_slabs]
    psel_ref, den_ref, pnew_ref, vn_ref, o_ref = refs[n_slabs:]
    acc = pnew_ref[...] * vn_ref[...] * (1.0 / PAGE_SIZE)
    for t in range(MOBA_TOPK):
        for r in range(PAGES_PER_BLOCK):
            acc = acc + slabs[t * PAGES_PER_BLOCK + r][...] * psel_ref[t:t + 1, r * PAGE_SIZE:(r + 1) * PAGE_SIZE]
    out = jnp.sum(acc, axis=1, keepdims=True) / den_ref[:, 0:1]
    o_ref[...] = jnp.broadcast_to(out, o_ref.shape)


def _sample_pv(page_table, ids, cache_vT, psel, den, pnew, v_new):
    db = page_table.shape[0]

    def slab_spec(t, r):
        def index(b, h, pt, ids):
            blk = jnp.clip(ids[(b * N_HEADS + h) * MOBA_TOPK + t], 0, pt.shape[1] // PAGES_PER_BLOCK - 1)
            return (0, pt[b, blk * PAGES_PER_BLOCK + r], h, 0, 0)
        return pl.BlockSpec((None, None, None, HEAD_DIM, PAGE_SIZE), index)

    slab_specs = [slab_spec(t, r) for t in range(MOBA_TOPK) for r in range(PAGES_PER_BLOCK)]
    per_head = lambda *tail: pl.BlockSpec((None, None) + tail, lambda b, h, pt, ids: (b, h) + (0,) * len(tail))
    grid_spec = pltpu.PrefetchScalarGridSpec(
        num_scalar_prefetch=2,
        grid=(db, N_HEADS),
        in_specs=slab_specs + [per_head(MOBA_TOPK, MOBA_BLOCK), per_head(1, LANES), per_head(1, LANES),
                               per_head(HEAD_DIM, PAGE_SIZE)],
        out_specs=per_head(HEAD_DIM, LANES),
    )
    return pl.pallas_call(
        _sample_pv_kernel,
        grid_spec=grid_spec,
        out_shape=jax.ShapeDtypeStruct((db, N_HEADS, HEAD_DIM, LANES), F32),
        compiler_params=pltpu.CompilerParams(
            dimension_semantics=("arbitrary", "arbitrary"), vmem_limit_bytes=VMEM_LIMIT),
        name="sample_pv",
    )(page_table, ids, *([cache_vT] * len(slab_specs)), psel, den, pnew, v_new)


def _sample_tail_kernel(x_ref, att_ref, zb_ref, u_ref, state_ref, za_ref, ga_ref, gb_ref, p_ref,
                        wmix_ref, scale_ref, wpo_ref, wao_ref, wo_ref, g_ref, b_ref, wple_ref, wgate_ref,
                        y_ref):
    u = u_ref[...]
    diffs = []
    for g, w in enumerate(POOL_WINDOWS):
        lanes = slice(g * POOL_GROUP, (g + 1) * POOL_GROUP)
        total = u[:, lanes]
        for back in range(1, w):
            total = total + state_ref[POOL_STATE - back, :, lanes]
        diffs.append(total / float(w) - u[:, lanes])
    pool = _pool_mix(diffs, wmix_ref, scale_ref[...])
    y_ref[...] = _layer_tail(x_ref[...], att_ref[...], zb_ref[...], pool, za_ref[...], ga_ref[...],
                             gb_ref[...], p_ref[...], wpo_ref, wao_ref, wo_ref, g_ref, b_ref,
                             wple_ref, wgate_ref)


def _sample_tail(x, att, zb, u, state_t, za, ga, gb, p, weights):
    return pl.pallas_call(
        _sample_tail_kernel,
        out_shape=jax.ShapeDtypeStruct(x.shape, F32),
        compiler_params=pltpu.CompilerParams(vmem_limit_bytes=VMEM_LIMIT),
        name="sample_tail",
    )(x, att, zb, u, state_t, za, ga, gb, p, *weights)


def _rope_tables(pos):
    inv = ROPE_THETA ** (-jnp.arange(HALF_DIM, dtype=F32) / HALF_DIM)
    ang = pos.astype(F32)[:, None] * inv[None, :]
    cos = jnp.cos(ang)
    sin = jnp.sin(ang)
    return jnp.tile(cos, (1, 4)), jnp.tile(jnp.concatenate([-sin, sin], axis=1), (1, 2))


def kernel(x_prompt, x_sample, cache_k, cache_v, state_pool, page_table, p_prompt, p_sample, w_in, w_pool_mix, pool_scale, w_pool_out, w_att_out, w_o, ln_g, ln_b, w_ple, w_ple_gate):
    assert w_in.shape[0] == DEPTH
    b, s, _ = x_prompt.shape
    db, n, _ = x_sample.shape
    assert n == 1
    n_pages = page_table.shape[1]
    past_len = n_pages * PAGE_SIZE
    assert past_len % MOBA_BLOCK == 0 and past_len // MOBA_BLOCK >= MOBA_TOPK
    nb = s // MOBA_BLOCK

    w_in_b = w_in[0].astype(BF16)
    weights = (w_pool_mix[0].astype(BF16), pool_scale[0][None, :], w_pool_out[0].astype(BF16),
               w_att_out[0].astype(BF16), w_o[0].astype(BF16), ln_g[0][None, :], ln_b[0][None, :],
               w_ple[0].astype(BF16), w_ple_gate[0].astype(BF16))

    cos_p, sin_p = _rope_tables(jnp.arange(s))
    qT, kx, vT, k_p, v_p, kmean, zb, u_p, za, ga, gb = _prompt_inproj(x_prompt, w_in_b, cos_p, sin_p)
    kmean_h = kmean.reshape(b, nb, N_HEADS, HEAD_DIM).transpose(0, 2, 1, 3)
    attT = _prompt_attn(qT, kmean_h, kx, vT)
    y_prompt = _prompt_tail(x_prompt, attT, zb, u_p, za, ga, gb, p_prompt[0], weights)

    cos_s, sin_s = _rope_tables(jnp.full((1,), past_len))
    q_s, k_s, v_s, zb_s, u_s, za_s, ga_s, gb_s = _sample_inproj(x_sample[:, 0], w_in_b, cos_s, sin_s)
    cache_kT = cache_k.transpose(0, 1, 3, 4, 2)
    cache_vT = cache_v.transpose(0, 1, 3, 4, 2)
    column = lambda t: jnp.broadcast_to(t.reshape(db, N_HEADS, HEAD_DIM, 1), (db, N_HEADS, HEAD_DIM, PAGE_SIZE))
    psel, ids, den, pnew = _sample_probs(page_table, cache_kT, column(q_s), column(k_s))
    att_s = _sample_pv(page_table, ids[..., 0].reshape(-1), cache_vT, psel, den, pnew, column(v_s))[..., 0]
    state = state_pool[0]
    y_sample = _sample_tail(x_sample[:, 0], att_s.reshape(db, ATT_WIDTH), zb_s, u_s, state.transpose(1, 0, 2),
                            za_s, ga_s, gb_s, p_sample[0][:, 0], weights)

    heads = (N_HEADS, HEAD_DIM)
    pool_prompt = u_p[:, s - POOL_STATE:]
    pool_sample = jnp.concatenate([state[:, 1:], u_s[:, None, :]], axis=1)
    return (y_prompt, y_sample[:, None, :],
            k_p.reshape(1, b, s, *heads), v_p.reshape(1, b, s, *heads), pool_prompt[None],
            k_s.reshape(1, db, 1, *heads), v_s.reshape(1, db, 1, *heads), pool_sample[None])
```

```python
import functools

import numpy as np
import jax
import jax.numpy as jnp
from jax import lax
from jax.experimental import pallas as pl
from jax.experimental.pallas import tpu as pltpu

D_MODEL = 1024
N_HEADS = 8
HEAD_DIM = 64
HALF_DIM = HEAD_DIM // 2
ATT_WIDTH = N_HEADS * HEAD_DIM
POOL_WINDOWS = (2, 4, 8, 16)
POOL_WIDTH = D_MODEL // 2
POOL_GROUP = POOL_WIDTH // len(POOL_WINDOWS)
POOL_STATE = max(POOL_WINDOWS) - 1
MOBA_BLOCK = 256
MOBA_TOPK = 3
PAGE_SIZE = 128
PAGES_PER_BLOCK = MOBA_BLOCK // PAGE_SIZE
ROPE_THETA = 10000.0
PLE_DIM = 256
LN_EPS = 1e-5
DEPTH = 1
DEEPNORM_ALPHA = (2 * DEPTH) ** 0.25
IN_SPLITS = (ATT_WIDTH, ATT_WIDTH, ATT_WIDTH, ATT_WIDTH, POOL_WIDTH, POOL_WIDTH, D_MODEL, D_MODEL)
IN_OFFS = tuple(int(v) for v in np.cumsum((0,) + IN_SPLITS))
IN_WIDTH = IN_OFFS[-1]
Q_SCALE = HEAD_DIM ** -0.5 * float(np.log2(np.e))
SUM_ROWS = 16

LANES = 128
HIST_ROWS = 16
MASK_NEG = -1e30
ROW_TILE = 256
VMEM_LIMIT = 52 * 1024 * 1024
ATTN_VMEM_LIMIT = 57 * 1024 * 1024

F32 = jnp.float32
BF16 = jnp.bfloat16


def _sigmoid(x):
    return 1.0 / (1.0 + jnp.exp(-x))


def _silu(x):
    return x * _sigmoid(x)


def _dot(a, b):
    return jnp.dot(a, b, preferred_element_type=F32)


def _rope(t, cos, sin_signed):
    lane = lax.broadcasted_iota(jnp.int32, (1, LANES), 1)
    first_half = (lane % HEAD_DIM) < HALF_DIM
    outs = []
    for c in range(t.shape[1] // LANES):
        tc = t[:, c * LANES:(c + 1) * LANES]
        partner = jnp.where(first_half, pltpu.roll(tc, LANES - HALF_DIM, 1), pltpu.roll(tc, HALF_DIM, 1))
        outs.append(tc * cos + partner * sin_signed)
    return jnp.concatenate(outs, axis=1)


def _project(xb, w_ref, seg):
    return _dot(xb, w_ref[:, IN_OFFS[seg]:IN_OFFS[seg + 1]])


def _prompt_inproj_kernel(x_ref, w_ref, cos_ref, sin_ref,
                          qT_ref, kx_ref, vT_ref, k_ref, v_ref, km_ref,
                          zb_ref, u_ref, za_ref, ga_ref, gb_ref):
    i = pl.program_id(1)
    xb = x_ref[...].astype(BF16)
    cos = cos_ref[...]
    sin = sin_ref[...]

    q = _rope(_project(xb, w_ref, 0), cos, sin) * Q_SCALE
    qT_ref[...] = q.T.astype(BF16)

    k = _rope(_project(xb, w_ref, 1), cos, sin)
    k_ref[...] = k
    km_ref[...] = jnp.sum(k, axis=0, keepdims=True) / MOBA_BLOCK
    lane = lax.broadcasted_iota(jnp.int32, (ROW_TILE, LANES), 1)
    extra = jnp.where(lane - HEAD_DIM == i, 1.0, 0.0)
    for h in range(N_HEADS):
        pair = k[:, (h // 2) * LANES:(h // 2 + 1) * LANES]
        base = pair if h % 2 == 0 else pltpu.roll(pair, HEAD_DIM, 1)
        kx_ref[h, 0] = jnp.where(lane < HEAD_DIM, base, extra).astype(BF16)

    v = _project(xb, w_ref, 2)
    v_ref[...] = v
    vT = v.T.astype(BF16).reshape(N_HEADS, HEAD_DIM, ROW_TILE)
    vT_ref[:, 0] = jnp.concatenate([vT, jnp.ones((N_HEADS, SUM_ROWS, ROW_TILE), BF16)], axis=1)

    zb_ref[...] = _project(xb, w_ref, 3)
    u_ref[...] = _project(xb, w_ref, 4)
    za_ref[...] = _project(xb, w_ref, 5)
    ga_ref[...] = _project(xb, w_ref, 6)
    gb_ref[...] = _project(xb, w_ref, 7)


def _prompt_inproj(x, w_in, cos, sin):
    b, s, _ = x.shape
    nb = s // MOBA_BLOCK
    assert ROW_TILE == MOBA_BLOCK and s % ROW_TILE == 0 and nb <= 32
    row = lambda width: pl.BlockSpec((None, ROW_TILE, width), lambda bi, i: (bi, i, 0))
    out_shape = [
        jax.ShapeDtypeStruct((b, ATT_WIDTH, s), BF16),
        jax.ShapeDtypeStruct((b, N_HEADS, nb, MOBA_BLOCK, LANES), BF16),
        jax.ShapeDtypeStruct((b, N_HEADS, nb, HEAD_DIM + SUM_ROWS, MOBA_BLOCK), BF16),
        jax.ShapeDtypeStruct((b, s, ATT_WIDTH), F32),
        jax.ShapeDtypeStruct((b, s, ATT_WIDTH), F32),
        jax.ShapeDtypeStruct((b, nb, 1, ATT_WIDTH), F32),
        jax.ShapeDtypeStruct((b, s, ATT_WIDTH), F32),
        jax.ShapeDtypeStruct((b, s, POOL_WIDTH), F32),
        jax.ShapeDtypeStruct((b, s, POOL_WIDTH), F32),
        jax.ShapeDtypeStruct((b, s, D_MODEL), F32),
        jax.ShapeDtypeStruct((b, s, D_MODEL), F32),
    ]
    out_specs = [
        pl.BlockSpec((None, ATT_WIDTH, ROW_TILE), lambda bi, i: (bi, 0, i)),
        pl.BlockSpec((None, N_HEADS, 1, MOBA_BLOCK, LANES), lambda bi, i: (bi, 0, i, 0, 0)),
        pl.BlockSpec((None, N_HEADS, 1, HEAD_DIM + SUM_ROWS, MOBA_BLOCK), lambda bi, i: (bi, 0, i, 0, 0)),
        row(ATT_WIDTH), row(ATT_WIDTH),
        pl.BlockSpec((None, None, 1, ATT_WIDTH), lambda bi, i: (bi, i, 0, 0)),
        row(ATT_WIDTH), row(POOL_WIDTH), row(POOL_WIDTH), row(D_MODEL), row(D_MODEL),
    ]
    in_specs = [
        row(D_MODEL),
        pl.BlockSpec((D_MODEL, IN_WIDTH), lambda bi, i: (0, 0)),
        pl.BlockSpec((ROW_TILE, LANES), lambda bi, i: (i, 0)),
        pl.BlockSpec((ROW_TILE, LANES), lambda bi, i: (i, 0)),
    ]
    return pl.pallas_call(
        _prompt_inproj_kernel,
        grid=(b, s // ROW_TILE),
        in_specs=in_specs, out_specs=out_specs, out_shape=out_shape,
        compiler_params=pltpu.CompilerParams(
            dimension_semantics=("arbitrary", "arbitrary"), vmem_limit_bytes=VMEM_LIMIT),
        name="prompt_inproj",
    )(x, w_in, cos, sin)


def _select_bias(scores, n_past, own):
    blk = lax.broadcasted_iota(jnp.int32, scores.shape, 0)
    n_blk = scores.shape[0]
    s = jnp.where(blk < n_past, scores, -jnp.inf)
    chosen = blk == own
    for _ in range(MOBA_TOPK):
        m = jnp.max(s, axis=0, keepdims=True)
        cand = jnp.where((s == m) & (m > -jnp.inf), blk, n_blk)
        first = jnp.min(cand, axis=0, keepdims=True)
        pick = blk == first
        chosen = chosen | pick
        s = jnp.where(pick, -jnp.inf, s)
    return jnp.where(chosen, 0.0, MASK_NEG)


def _sample_logits_stage(pt_ref, qcol_ref, pool_ref, logit_ref, page_buf, page_sem):
    n_steps = pl.num_programs(0) * pl.num_programs(1)
    g = pl.program_id(0) * pl.num_programs(1) + pl.program_id(1)
    pages_per_step = page_buf.shape[0]
    steps_per_seq = pt_ref.shape[1] // pages_per_step

    def page_copy(step, r):
        seq = step // steps_per_seq
        first = (step % steps_per_seq) * pages_per_step
        return pltpu.make_async_copy(pool_ref.at[0, pt_ref[seq, first + r]], page_buf.at[r], page_sem)

    @pl.when(g == 0)
    def _():
        for r in range(pages_per_step):
            page_copy(0, r).start()

    for r in range(pages_per_step):
        page_copy(g, r).wait()

    rows_per_store = 8
    for h in range(N_HEADS):
        qh = qcol_ref[h]
        for first_blk in range(0, pages_per_step // PAGES_PER_BLOCK, rows_per_store):
            rows = []
            for blk in range(first_blk, first_blk + rows_per_store):
                halves = [jnp.sum(page_buf[blk * PAGES_PER_BLOCK + r, h] * qh, axis=0, keepdims=True)
                          for r in range(PAGES_PER_BLOCK)]
                rows.append(jnp.concatenate(halves, axis=1))
            logit_ref[h, first_blk:first_blk + rows_per_store, :] = jnp.concatenate(rows, axis=0)

    @pl.when(g + 1 < n_steps)
    def _():
        for r in range(pages_per_step):
            page_copy(g + 1, r).start()


def _prompt_attn_kernel(pt_ref, qT_ref, km_ref, kx_ref, vT_ref, qcol_ref, pool_ref, o_ref, logit_ref,
                        qx_scr, s_scr, p_scr, acc_scr, m_scr, alpha_scr, page_buf, page_sem):
    _sample_logits_stage(pt_ref, qcol_ref, pool_ref, logit_ref, page_buf, page_sem)

    i = pl.program_id(1)
    n_blk = km_ref.shape[1]
    heads = range(N_HEADS)

    pad = jnp.zeros((LANES - HEAD_DIM - n_blk, MOBA_BLOCK), BF16)
    for h in heads:
        qT = qT_ref[h * HEAD_DIM:(h + 1) * HEAD_DIM, :]
        bias = _select_bias(_dot(km_ref[h].astype(BF16), qT), i, i)
        qx_scr[h] = jnp.concatenate([qT, bias.astype(BF16), pad], axis=0)
        m_scr[h] = jnp.full((1, MOBA_BLOCK), MASK_NEG, F32)
        acc_scr[h] = jnp.zeros((HEAD_DIM + SUM_ROWS, MOBA_BLOCK), F32)

    def scores(h, blk):
        s_scr[h] = _dot(kx_ref[h, blk], qx_scr[h])

    def softmax(h, keep=None):
        for c in range(MOBA_BLOCK // LANES):
            lanes = slice(c * LANES, (c + 1) * LANES)
            s = s_scr[h, :, lanes]
            if keep is not None:
                s = jnp.where(keep[:, lanes], s, MASK_NEG)
            m_old = m_scr[h, :, lanes]
            m_new = jnp.maximum(m_old, jnp.max(s, axis=0, keepdims=True))
            m_scr[h, :, lanes] = m_new
            alpha_scr[h, :, lanes] = jnp.exp2(m_old - m_new)
            p_scr[h, :, lanes] = jnp.exp2(s - m_new).astype(BF16)

    def accumulate(h, blk):
        acc_scr[h] = alpha_scr[h] * acc_scr[h] + _dot(vT_ref[h, blk], p_scr[h])

    key_row = lax.broadcasted_iota(jnp.int32, (MOBA_BLOCK, MOBA_BLOCK), 0)
    qry_col = lax.broadcasted_iota(jnp.int32, (MOBA_BLOCK, MOBA_BLOCK), 1)
    causal = key_row <= qry_col
    for h in heads:
        scores(h, i)
    for h in heads:
        softmax(h, causal)
    for h in heads:
        scores(h, 0)

    def body(t, _):
        blk_prev = jnp.where(t == 1, i, t - 2)
        for h in heads:
            accumulate(h, blk_prev)
        for h in heads:
            softmax(h)
        for h in heads:
            scores(h, t)
        return 0

    lax.fori_loop(1, i + 1, body, 0)
    blk_last = jnp.maximum(i - 1, 0)
    for h in heads:
        accumulate(h, blk_last)
        acc = acc_scr[h]
        o_ref[h * HEAD_DIM:(h + 1) * HEAD_DIM, :] = acc[:HEAD_DIM] / acc[HEAD_DIM:HEAD_DIM + 1]


def _prompt_attn(qT, kmean, kx, vT, page_table, q_col, cache_kT):
    b, _, s = qT.shape
    nb = s // MOBA_BLOCK
    db, n_pages = page_table.shape
    steps_per_seq, rem = divmod(b * nb, db)
    assert rem == 0 and n_pages % (steps_per_seq * PAGES_PER_BLOCK * 8) == 0
    pages_per_step = n_pages // steps_per_seq
    blocks_per_step = pages_per_step // PAGES_PER_BLOCK
    once = pl.Buffered(1)
    seq_of = lambda bi, i: (bi * nb + i) // steps_per_seq
    grid_spec = pltpu.PrefetchScalarGridSpec(
        num_scalar_prefetch=1,
        grid=(b, nb),
        in_specs=[
            pl.BlockSpec((None, ATT_WIDTH, MOBA_BLOCK), lambda bi, i, pt: (bi, 0, i)),
            pl.BlockSpec((None, N_HEADS, nb, HEAD_DIM), lambda bi, i, pt: (bi, 0, 0, 0)),
            pl.BlockSpec((None, N_HEADS, nb, MOBA_BLOCK, LANES), lambda bi, i, pt: (bi, 0, 0, 0, 0),
                         pipeline_mode=once),
            pl.BlockSpec((None, N_HEADS, nb, HEAD_DIM + SUM_ROWS, MOBA_BLOCK), lambda bi, i, pt: (bi, 0, 0, 0, 0),
                         pipeline_mode=once),
            pl.BlockSpec((None, N_HEADS, HEAD_DIM, PAGE_SIZE), lambda bi, i, pt: (seq_of(bi, i), 0, 0, 0)),
            pl.BlockSpec(memory_space=pl.ANY),
        ],
        out_specs=[
            pl.BlockSpec((None, ATT_WIDTH, MOBA_BLOCK), lambda bi, i, pt: (bi, 0, i)),
            pl.BlockSpec((None, N_HEADS, blocks_per_step, MOBA_BLOCK),
                         lambda bi, i, pt: (seq_of(bi, i), 0, (bi * nb + i) % steps_per_seq, 0)),
        ],
        scratch_shapes=[
            pltpu.VMEM((N_HEADS, LANES, MOBA_BLOCK), BF16),
            pltpu.VMEM((N_HEADS, MOBA_BLOCK, MOBA_BLOCK), F32),
            pltpu.VMEM((N_HEADS, MOBA_BLOCK, MOBA_BLOCK), BF16),
            pltpu.VMEM((N_HEADS, HEAD_DIM + SUM_ROWS, MOBA_BLOCK), F32),
            pltpu.VMEM((N_HEADS, 1, MOBA_BLOCK), F32),
            pltpu.VMEM((N_HEADS, 1, MOBA_BLOCK), F32),
            pltpu.VMEM((pages_per_step, N_HEADS, HEAD_DIM, PAGE_SIZE), F32),
            pltpu.SemaphoreType.DMA,
        ],
    )
    return pl.pallas_call(
        _prompt_attn_kernel,
        grid_spec=grid_spec,
        out_shape=[jax.ShapeDtypeStruct((b, ATT_WIDTH, s), F32),
                   jax.ShapeDtypeStruct((db, N_HEADS, n_pages // PAGES_PER_BLOCK, MOBA_BLOCK), F32)],
        compiler_params=pltpu.CompilerParams(
            dimension_semantics=("arbitrary", "arbitrary"), vmem_limit_bytes=ATTN_VMEM_LIMIT),
        name="prompt_attn",
    )(page_table, qT, kmean, kx, vT, q_col, cache_kT)


def _pool_mix(diffs, wmix_ref, scale):
    mixed = [_dot(d.astype(BF16), wmix_ref[g]) for g, d in enumerate(diffs)]
    return jnp.concatenate(mixed, axis=1) * scale


def _layer_tail(x, att, zb, pool, za, ga, gb, p, wpo_ref, wao_ref, wo_ref, g_ref, b_ref, wple_ref, wgate_ref):
    y_pool = _dot((pool * _silu(za)).astype(BF16), wpo_ref[...])
    y_att = _dot((att * _silu(zb)).astype(BF16), wao_ref[...])
    merged = _sigmoid(ga) * y_pool + _sigmoid(gb) * y_att
    r = DEEPNORM_ALPHA * x + _dot(merged.astype(BF16), wo_ref[...])
    mu = jnp.mean(r, axis=-1, keepdims=True)
    c = r - mu
    var = jnp.mean(c * c, axis=-1, keepdims=True)
    h = c * lax.rsqrt(var + LN_EPS) * g_ref[...] + b_ref[...]
    gate = _sigmoid(_dot(h.astype(BF16), wgate_ref[...]))
    return h + gate * _dot(p.astype(BF16), wple_ref[...])


def _prompt_tail_kernel(x_ref, attT_ref, zb_ref, u_ref, uprev_ref, za_ref, ga_ref, gb_ref, p_ref,
                        wmix_ref, scale_ref, wpo_ref, wao_ref, wo_ref, g_ref, b_ref, wple_ref, wgate_ref,
                        y_ref):
    i = pl.program_id(1)
    u = u_ref[...]
    hist = jnp.where(i > 0, uprev_ref[...], 0.0)
    ext = jnp.concatenate([hist, u], axis=0)
    pos = i * ROW_TILE + lax.broadcasted_iota(jnp.int32, (ROW_TILE, POOL_GROUP), 0)
    diffs = []
    for g, w in enumerate(POOL_WINDOWS):
        e = ext[:, g * POOL_GROUP:(g + 1) * POOL_GROUP]
        step = 1
        while step < w:
            e = e + pltpu.roll(e, step, 0)
            step *= 2
        cnt = jnp.minimum(pos + 1, w).astype(F32)
        diffs.append(e[HIST_ROWS:] / cnt - u[:, g * POOL_GROUP:(g + 1) * POOL_GROUP])
    pool = _pool_mix(diffs, wmix_ref, scale_ref[...])
    y_ref[...] = _layer_tail(x_ref[...], attT_ref[...].T, zb_ref[...], pool, za_ref[...], ga_ref[...],
                             gb_ref[...], p_ref[...], wpo_ref, wao_ref, wo_ref, g_ref, b_ref,
                             wple_ref, wgate_ref)


def _full(shape):
    return pl.BlockSpec(shape, lambda *_: (0,) * len(shape))


def _tail_weight_specs():
    return [
        _full((len(POOL_WINDOWS), POOL_GROUP, POOL_GROUP)), _full((1, POOL_WIDTH)),
        _full((POOL_WIDTH, D_MODEL)), _full((ATT_WIDTH, D_MODEL)), _full((D_MODEL, D_MODEL)),
        _full((1, D_MODEL)), _full((1, D_MODEL)), _full((PLE_DIM, D_MODEL)), _full((D_MODEL, D_MODEL)),
    ]


def _prompt_tail(x, attT, zb, u, za, ga, gb, p, weights):
    b, s, _ = x.shape
    row = lambda width: pl.BlockSpec((None, ROW_TILE, width), lambda bi, i: (bi, i, 0))
    hist_blocks = ROW_TILE // HIST_ROWS
    in_specs = [
        row(D_MODEL),
        pl.BlockSpec((None, ATT_WIDTH, ROW_TILE), lambda bi, i: (bi, 0, i)),
        row(ATT_WIDTH), row(POOL_WIDTH),
        pl.BlockSpec((None, HIST_ROWS, POOL_WIDTH), lambda bi, i: (bi, jnp.maximum(i * hist_blocks - 1, 0), 0)),
        row(POOL_WIDTH), row(D_MODEL), row(D_MODEL), row(PLE_DIM),
    ] + _tail_weight_specs()
    return pl.pallas_call(
        _prompt_tail_kernel,
        grid=(b, s // ROW_TILE),
        in_specs=in_specs,
        out_specs=row(D_MODEL),
        out_shape=jax.ShapeDtypeStruct((b, s, D_MODEL), F32),
        compiler_params=pltpu.CompilerParams(
            dimension_semantics=("arbitrary", "arbitrary"), vmem_limit_bytes=VMEM_LIMIT),
        name="prompt_tail",
    )(x, attT, zb, u, u, za, ga, gb, p, *weights)


def _sample_inproj_kernel(x_ref, w_ref, cos_ref, sin_ref,
                          q_ref, k_ref, v_ref, zb_ref, u_ref, za_ref, ga_ref, gb_ref):
    xb = x_ref[...].astype(BF16)
    cos = cos_ref[...]
    sin = sin_ref[...]
    q_ref[...] = _rope(_project(xb, w_ref, 0), cos, sin) * Q_SCALE
    k_ref[...] = _rope(_project(xb, w_ref, 1), cos, sin)
    v_ref[...] = _project(xb, w_ref, 2)
    zb_ref[...] = _project(xb, w_ref, 3)
    u_ref[...] = _project(xb, w_ref, 4)
    za_ref[...] = _project(xb, w_ref, 5)
    ga_ref[...] = _project(xb, w_ref, 6)
    gb_ref[...] = _project(xb, w_ref, 7)


def _sample_inproj(x, w_in, cos, sin):
    n = x.shape[0]
    widths = (ATT_WIDTH, ATT_WIDTH, ATT_WIDTH, ATT_WIDTH, POOL_WIDTH, POOL_WIDTH, D_MODEL, D_MODEL)
    return pl.pallas_call(
        _sample_inproj_kernel,
        out_shape=[jax.ShapeDtypeStruct((n, w), F32) for w in widths],
        compiler_params=pltpu.CompilerParams(vmem_limit_bytes=VMEM_LIMIT),
        name="sample_inproj",
    )(x, w_in, cos, sin)


def _sample_select_kernel(logit_ref, q_ref, kn_ref, psel_ref, ids_ref, den_ref, pnew_ref):
    logits = logit_ref[...]
    n_blk = logits.shape[1]
    sc = jnp.sum(logits, axis=2, keepdims=True) / MOBA_BLOCK
    blk = lax.broadcasted_iota(jnp.int32, sc.shape, 1)
    picks = []
    for t in range(MOBA_TOPK):
        m = jnp.max(sc, axis=1, keepdims=True)
        first = jnp.min(jnp.where(sc == m, blk, n_blk), axis=1, keepdims=True)
        ids_ref[:, t:t + 1, :] = jnp.broadcast_to(first, (N_HEADS, 1, LANES))
        picks.append(blk == first)
        sc = jnp.where(picks[-1], -jnp.inf, sc)
    chosen = picks[0]
    for pick in picks[1:]:
        chosen = chosen | pick
    s_new = jnp.sum(q_ref[...] * kn_ref[...], axis=1, keepdims=True)[:, :, 0:1]
    masked = jnp.where(chosen, logits, MASK_NEG)
    m = jnp.max(jnp.max(masked, axis=2, keepdims=True), axis=1, keepdims=True)
    m = jnp.maximum(m, s_new)
    p = jnp.exp2(masked - m)
    p_new = jnp.exp2(s_new - m)
    den = jnp.sum(jnp.sum(p, axis=2, keepdims=True), axis=1, keepdims=True) + p_new
    den_ref[...] = jnp.broadcast_to(den, den_ref.shape)
    pnew_ref[...] = jnp.broadcast_to(p_new, pnew_ref.shape)
    for t, pick in enumerate(picks):
        psel_ref[:, t:t + 1, :] = jnp.sum(jnp.where(pick, p, 0.0), axis=1, keepdims=True)


def _sample_select(logits, q, k_new):
    db, _, n_blk, _ = logits.shape
    per_seq = lambda *tail: pl.BlockSpec((None, N_HEADS) + tail, lambda b: (b,) + (0,) * (len(tail) + 1))
    return pl.pallas_call(
        _sample_select_kernel,
        grid=(db,),
        in_specs=[per_seq(n_blk, MOBA_BLOCK), per_seq(HEAD_DIM, PAGE_SIZE), per_seq(HEAD_DIM, PAGE_SIZE)],
        out_specs=[per_seq(MOBA_TOPK, MOBA_BLOCK), per_seq(MOBA_TOPK, LANES), per_seq(1, LANES), per_seq(1, LANES)],
        out_shape=[jax.ShapeDtypeStruct((db, N_HEADS, MOBA_TOPK, MOBA_BLOCK), F32),
                   jax.ShapeDtypeStruct((db, N_HEADS, MOBA_TOPK, LANES), jnp.int32),
                   jax.ShapeDtypeStruct((db, N_HEADS, 1, LANES), F32),
                   jax.ShapeDtypeStruct((db, N_HEADS, 1, LANES), F32)],
        compiler_params=pltpu.CompilerParams(dimension_semantics=("arbitrary",), vmem_limit_bytes=VMEM_LIMIT),
        name="sample_select",
    )(logits, q, k_new)


PV_HEADS = 4


def _sample_pv_kernel(pt_ref, ids_ref, *refs):
    per_head = MOBA_TOPK * PAGES_PER_BLOCK
    slabs = refs[:PV_HEADS * per_head]
    psel_ref, den_ref, pnew_ref, vn_ref, o_ref = refs[PV_HEADS * per_head:]
    for hh in range(PV_HEADS):
        acc = pnew_ref[hh] * vn_ref[hh] * (1.0 / PAGE_SIZE)
        for t in range(MOBA_TOPK):
            for r in range(PAGES_PER_BLOCK):
                slab = slabs[(hh * MOBA_TOPK + t) * PAGES_PER_BLOCK + r][...]
                acc = acc + slab * psel_ref[hh, t:t + 1, r * PAGE_SIZE:(r + 1) * PAGE_SIZE]
        out = jnp.sum(acc, axis=1, keepdims=True) / den_ref[hh, :, 0:1]
        o_ref[hh] = jnp.broadcast_to(out, o_ref.shape[1:])


def _sample_pv(page_table, ids, cache_vT, psel, den, pnew, v_new):
    db = page_table.shape[0]

    def slab_spec(hh, t, r):
        def index(b, hg, pt, ids):
            h = hg * PV_HEADS + hh
            blk = jnp.clip(ids[(b * N_HEADS + h) * MOBA_TOPK + t], 0, pt.shape[1] // PAGES_PER_BLOCK - 1)
            return (0, pt[b, blk * PAGES_PER_BLOCK + r], h, 0, 0)
        return pl.BlockSpec((None, None, None, HEAD_DIM, PAGE_SIZE), index)

    slab_specs = [slab_spec(hh, t, r) for hh in range(PV_HEADS) for t in range(MOBA_TOPK)
                  for r in range(PAGES_PER_BLOCK)]
    group = lambda *tail: pl.BlockSpec((None, PV_HEADS) + tail, lambda b, hg, pt, ids: (b, hg) + (0,) * len(tail))
    grid_spec = pltpu.PrefetchScalarGridSpec(
        num_scalar_prefetch=2,
        grid=(db, N_HEADS // PV_HEADS),
        in_specs=slab_specs + [group(MOBA_TOPK, MOBA_BLOCK), group(1, LANES), group(1, LANES),
                               group(HEAD_DIM, PAGE_SIZE)],
        out_specs=group(HEAD_DIM, LANES),
    )
    return pl.pallas_call(
        _sample_pv_kernel,
        grid_spec=grid_spec,
        out_shape=jax.ShapeDtypeStruct((db, N_HEADS, HEAD_DIM, LANES), F32),
        compiler_params=pltpu.CompilerParams(
            dimension_semantics=("arbitrary", "arbitrary"), vmem_limit_bytes=VMEM_LIMIT),
        name="sample_pv",
    )(page_table, ids, *([cache_vT] * len(slab_specs)), psel, den, pnew, v_new)


def _sample_tail_kernel(x_ref, att_ref, zb_ref, u_ref, state_ref, za_ref, ga_ref, gb_ref, p_ref,
                        wmix_ref, scale_ref, wpo_ref, wao_ref, wo_ref, g_ref, b_ref, wple_ref, wgate_ref,
                        y_ref):
    u = u_ref[...]
    diffs = []
    for g, w in enumerate(POOL_WINDOWS):
        lanes = slice(g * POOL_GROUP, (g + 1) * POOL_GROUP)
        total = u[:, lanes]
        for back in range(1, w):
            total = total + state_ref[POOL_STATE - back, :, lanes]
        diffs.append(total / float(w) - u[:, lanes])
    pool = _pool_mix(diffs, wmix_ref, scale_ref[...])
    y_ref[...] = _layer_tail(x_ref[...], att_ref[...], zb_ref[...], pool, za_ref[...], ga_ref[...],
                             gb_ref[...], p_ref[...], wpo_ref, wao_ref, wo_ref, g_ref, b_ref,
                             wple_ref, wgate_ref)


def _sample_tail(x, att, zb, u, state_t, za, ga, gb, p, weights):
    return pl.pallas_call(
        _sample_tail_kernel,
        out_shape=jax.ShapeDtypeStruct(x.shape, F32),
        compiler_params=pltpu.CompilerParams(vmem_limit_bytes=VMEM_LIMIT),
        name="sample_tail",
    )(x, att, zb, u, state_t, za, ga, gb, p, *weights)


def _rope_tables(pos):
    inv = ROPE_THETA ** (-jnp.arange(HALF_DIM, dtype=F32) / HALF_DIM)
    ang = pos.astype(F32)[:, None] * inv[None, :]
    cos = jnp.cos(ang)
    sin = jnp.sin(ang)
    return jnp.tile(cos, (1, 4)), jnp.tile(jnp.concatenate([-sin, sin], axis=1), (1, 2))


def kernel(x_prompt, x_sample, cache_k, cache_v, state_pool, page_table, p_prompt, p_sample, w_in, w_pool_mix, pool_scale, w_pool_out, w_att_out, w_o, ln_g, ln_b, w_ple, w_ple_gate):
    assert w_in.shape[0] == DEPTH
    b, s, _ = x_prompt.shape
    db, n, _ = x_sample.shape
    assert n == 1
    n_pages = page_table.shape[1]
    past_len = n_pages * PAGE_SIZE
    assert past_len % MOBA_BLOCK == 0 and past_len // MOBA_BLOCK >= MOBA_TOPK
    nb = s // MOBA_BLOCK

    w_in_b = w_in[0].astype(BF16)
    weights = (w_pool_mix[0].astype(BF16), pool_scale[0][None, :], w_pool_out[0].astype(BF16),
               w_att_out[0].astype(BF16), w_o[0].astype(BF16), ln_g[0][None, :], ln_b[0][None, :],
               w_ple[0].astype(BF16), w_ple_gate[0].astype(BF16))

    cos_s, sin_s = _rope_tables(jnp.full((1,), past_len))
    q_s, k_s, v_s, zb_s, u_s, za_s, ga_s, gb_s = _sample_inproj(x_sample[:, 0], w_in_b, cos_s, sin_s)
    cache_kT = cache_k.transpose(0, 1, 3, 4, 2)
    cache_vT = cache_v.transpose(0, 1, 3, 4, 2)
    column = lambda t: jnp.broadcast_to(t.reshape(db, N_HEADS, HEAD_DIM, 1), (db, N_HEADS, HEAD_DIM, PAGE_SIZE))

    cos_p, sin_p = _rope_tables(jnp.arange(s))
    qT, kx, vT, k_p, v_p, kmean, zb, u_p, za, ga, gb = _prompt_inproj(x_prompt, w_in_b, cos_p, sin_p)
    kmean_h = kmean.reshape(b, nb, N_HEADS, HEAD_DIM).transpose(0, 2, 1, 3)
    attT, logits_s = _prompt_attn(qT, kmean_h, kx, vT, page_table, column(q_s), cache_kT)
    y_prompt = _prompt_tail(x_prompt, attT, zb, u_p, za, ga, gb, p_prompt[0], weights)

    psel, ids, den, pnew = _sample_select(logits_s, column(q_s), column(k_s))
    att_s = _sample_pv(page_table, ids[..., 0].reshape(-1), cache_vT, psel, den, pnew, column(v_s))[..., 0]
    state = state_pool[0]
    y_sample = _sample_tail(x_sample[:, 0], att_s.reshape(db, ATT_WIDTH), zb_s, u_s, state.transpose(1, 0, 2),
                            za_s, ga_s, gb_s, p_sample[0][:, 0], weights)

    heads = (N_HEADS, HEAD_DIM)
    pool_prompt = u_p[:, s - POOL_STATE:]
    pool_sample = jnp.concatenate([state[:, 1:], u_s[:, None, :]], axis=1)
    return (y_prompt, y_sample[:, None, :],
            k_p.reshape(1, b, s, *heads), v_p.reshape(1, b, s, *heads), pool_prompt[None],
            k_s.reshape(1, db, 1, *heads), v_s.reshape(1, db, 1, *heads), pool_sample[None])
```

```python
import functools

import numpy as np
import jax
import jax.numpy as jnp
from jax import lax
from jax.experimental import pallas as pl
from jax.experimental.pallas import tpu as pltpu

D_MODEL = 1024
N_HEADS = 8
HEAD_DIM = 64
HALF_DIM = HEAD_DIM // 2
ATT_WIDTH = N_HEADS * HEAD_DIM
POOL_WINDOWS = (2, 4, 8, 16)
POOL_WIDTH = D_MODEL // 2
POOL_GROUP = POOL_WIDTH // len(POOL_WINDOWS)
POOL_STATE = max(POOL_WINDOWS) - 1
MOBA_BLOCK = 256
MOBA_TOPK = 3
PAGE_SIZE = 128
PAGES_PER_BLOCK = MOBA_BLOCK // PAGE_SIZE
ROPE_THETA = 10000.0
PLE_DIM = 256
LN_EPS = 1e-5
DEPTH = 1
DEEPNORM_ALPHA = (2 * DEPTH) ** 0.25
IN_SPLITS = (ATT_WIDTH, ATT_WIDTH, ATT_WIDTH, ATT_WIDTH, POOL_WIDTH, POOL_WIDTH, D_MODEL, D_MODEL)
IN_OFFS = tuple(int(v) for v in np.cumsum((0,) + IN_SPLITS))
IN_WIDTH = IN_OFFS[-1]
Q_SCALE = HEAD_DIM ** -0.5 * float(np.log2(np.e))
SUM_ROWS = 16

LANES = 128
HIST_ROWS = 16
MASK_NEG = -1e30
ROW_TILE = 256
TAIL_TILE = 512
TAIL_PART = 256
VMEM_LIMIT = 52 * 1024 * 1024
ATTN_VMEM_LIMIT = 57 * 1024 * 1024

F32 = jnp.float32
BF16 = jnp.bfloat16


def _sigmoid(x):
    return 0.5 * jnp.tanh(0.5 * x) + 0.5


def _silu(x):
    return x * _sigmoid(x)


def _dot(a, b):
    return jnp.dot(a, b, preferred_element_type=F32)


def _rope(t, cos, sin_signed):
    lane = lax.broadcasted_iota(jnp.int32, (1, LANES), 1)
    first_half = (lane % HEAD_DIM) < HALF_DIM
    outs = []
    for c in range(t.shape[1] // LANES):
        tc = t[:, c * LANES:(c + 1) * LANES]
        partner = jnp.where(first_half, pltpu.roll(tc, LANES - HALF_DIM, 1), pltpu.roll(tc, HALF_DIM, 1))
        outs.append(tc * cos + partner * sin_signed)
    return jnp.concatenate(outs, axis=1)


def _project(xb, w_ref, seg):
    return _dot(xb, w_ref[:, IN_OFFS[seg]:IN_OFFS[seg + 1]])


def _prompt_inproj_kernel(x_ref, w_ref, cos_ref, sin_ref,
                          qT_ref, kx_ref, vT_ref, k_ref, v_ref, km_ref,
                          zb_ref, u_ref, za_ref, ga_ref, gb_ref):
    i = pl.program_id(1)
    xb = x_ref[...].astype(BF16)
    cos = cos_ref[...]
    sin = sin_ref[...]

    q = _rope(_project(xb, w_ref, 0), cos, sin) * Q_SCALE
    qT_ref[...] = q.T.astype(BF16)

    k = _rope(_project(xb, w_ref, 1), cos, sin)
    k_ref[...] = k
    km_ref[...] = jnp.sum(k, axis=0, keepdims=True) / MOBA_BLOCK
    lane = lax.broadcasted_iota(jnp.int32, (ROW_TILE, LANES), 1)
    extra = jnp.where(lane - HEAD_DIM == i, 1.0, 0.0)
    for h in range(N_HEADS):
        pair = k[:, (h // 2) * LANES:(h // 2 + 1) * LANES]
        base = pair if h % 2 == 0 else pltpu.roll(pair, HEAD_DIM, 1)
        kx_ref[h, 0] = jnp.where(lane < HEAD_DIM, base, extra).astype(BF16)

    v = _project(xb, w_ref, 2)
    v_ref[...] = v
    vT = v.T.astype(BF16).reshape(N_HEADS, HEAD_DIM, ROW_TILE)
    vT_ref[:, 0] = jnp.concatenate([vT, jnp.ones((N_HEADS, SUM_ROWS, ROW_TILE), BF16)], axis=1)

    zb_ref[...] = _project(xb, w_ref, 3)
    u_ref[...] = _project(xb, w_ref, 4)
    za_ref[...] = _project(xb, w_ref, 5)
    ga_ref[...] = _project(xb, w_ref, 6)
    gb_ref[...] = _project(xb, w_ref, 7)


def _prompt_inproj(x, w_in, cos, sin):
    b, s, _ = x.shape
    nb = s // MOBA_BLOCK
    assert ROW_TILE == MOBA_BLOCK and s % ROW_TILE == 0 and nb <= 32
    row = lambda width: pl.BlockSpec((None, ROW_TILE, width), lambda bi, i: (bi, i, 0))
    out_shape = [
        jax.ShapeDtypeStruct((b, ATT_WIDTH, s), BF16),
        jax.ShapeDtypeStruct((b, N_HEADS, nb, MOBA_BLOCK, LANES), BF16),
        jax.ShapeDtypeStruct((b, N_HEADS, nb, HEAD_DIM + SUM_ROWS, MOBA_BLOCK), BF16),
        jax.ShapeDtypeStruct((b, s, ATT_WIDTH), F32),
        jax.ShapeDtypeStruct((b, s, ATT_WIDTH), F32),
        jax.ShapeDtypeStruct((b, nb, 1, ATT_WIDTH), F32),
        jax.ShapeDtypeStruct((b, s, ATT_WIDTH), F32),
        jax.ShapeDtypeStruct((b, s, POOL_WIDTH), F32),
        jax.ShapeDtypeStruct((b, s, POOL_WIDTH), F32),
        jax.ShapeDtypeStruct((b, s, D_MODEL), F32),
        jax.ShapeDtypeStruct((b, s, D_MODEL), F32),
    ]
    out_specs = [
        pl.BlockSpec((None, ATT_WIDTH, ROW_TILE), lambda bi, i: (bi, 0, i)),
        pl.BlockSpec((None, N_HEADS, 1, MOBA_BLOCK, LANES), lambda bi, i: (bi, 0, i, 0, 0)),
        pl.BlockSpec((None, N_HEADS, 1, HEAD_DIM + SUM_ROWS, MOBA_BLOCK), lambda bi, i: (bi, 0, i, 0, 0)),
        row(ATT_WIDTH), row(ATT_WIDTH),
        pl.BlockSpec((None, None, 1, ATT_WIDTH), lambda bi, i: (bi, i, 0, 0)),
        row(ATT_WIDTH), row(POOL_WIDTH), row(POOL_WIDTH), row(D_MODEL), row(D_MODEL),
    ]
    in_specs = [
        row(D_MODEL),
        pl.BlockSpec((D_MODEL, IN_WIDTH), lambda bi, i: (0, 0)),
        pl.BlockSpec((ROW_TILE, LANES), lambda bi, i: (i, 0)),
        pl.BlockSpec((ROW_TILE, LANES), lambda bi, i: (i, 0)),
    ]
    return pl.pallas_call(
        _prompt_inproj_kernel,
        grid=(b, s // ROW_TILE),
        in_specs=in_specs, out_specs=out_specs, out_shape=out_shape,
        compiler_params=pltpu.CompilerParams(
            dimension_semantics=("arbitrary", "arbitrary"), vmem_limit_bytes=VMEM_LIMIT),
        name="prompt_inproj",
    )(x, w_in, cos, sin)


def _select_bias(scores, n_past, own):
    blk = lax.broadcasted_iota(jnp.int32, scores.shape, 0)
    n_blk = scores.shape[0]
    s = jnp.where(blk < n_past, scores, -jnp.inf)
    chosen = blk == own
    for _ in range(MOBA_TOPK):
        m = jnp.max(s, axis=0, keepdims=True)
        cand = jnp.where((s == m) & (m > -jnp.inf), blk, n_blk)
        first = jnp.min(cand, axis=0, keepdims=True)
        pick = blk == first
        chosen = chosen | pick
        s = jnp.where(pick, -jnp.inf, s)
    return jnp.where(chosen, 0.0, MASK_NEG)


def _sample_logits_stage(pt_ref, qcol_ref, pool_ref, logit_ref, page_buf, page_sem):
    n_steps = pl.num_programs(0) * pl.num_programs(1)
    g = pl.program_id(0) * pl.num_programs(1) + pl.program_id(1)
    pages_per_step = page_buf.shape[0]
    steps_per_seq = pt_ref.shape[1] // pages_per_step

    def page_copy(step, r):
        seq = step // steps_per_seq
        first = (step % steps_per_seq) * pages_per_step
        return pltpu.make_async_copy(pool_ref.at[0, pt_ref[seq, first + r]], page_buf.at[r], page_sem)

    @pl.when(g == 0)
    def _():
        for r in range(pages_per_step):
            page_copy(0, r).start()

    for r in range(pages_per_step):
        page_copy(g, r).wait()

    rows_per_store = 8
    for h in range(N_HEADS):
        qh = qcol_ref[h]
        for first_blk in range(0, pages_per_step // PAGES_PER_BLOCK, rows_per_store):
            rows = []
            for blk in range(first_blk, first_blk + rows_per_store):
                halves = [jnp.sum(page_buf[blk * PAGES_PER_BLOCK + r, h] * qh, axis=0, keepdims=True)
                          for r in range(PAGES_PER_BLOCK)]
                rows.append(jnp.concatenate(halves, axis=1))
            logit_ref[h, first_blk:first_blk + rows_per_store, :] = jnp.concatenate(rows, axis=0)

    @pl.when(g + 1 < n_steps)
    def _():
        for r in range(pages_per_step):
            page_copy(g + 1, r).start()


def _prompt_attn_kernel(pt_ref, qT_ref, km_ref, kx_ref, vT_ref, qcol_ref, pool_ref, o_ref, logit_ref,
                        qx_scr, s_scr, p_scr, acc_scr, m_scr, alpha_scr, page_buf, page_sem):
    _sample_logits_stage(pt_ref, qcol_ref, pool_ref, logit_ref, page_buf, page_sem)

    i = pl.program_id(1)
    n_blk = km_ref.shape[1]
    heads = range(N_HEADS)

    pad = jnp.zeros((LANES - HEAD_DIM - n_blk, MOBA_BLOCK), BF16)
    for h in heads:
        qT = qT_ref[h * HEAD_DIM:(h + 1) * HEAD_DIM, :]
        bias = _select_bias(_dot(km_ref[h].astype(BF16), qT), i, i)
        qx_scr[h] = jnp.concatenate([qT, bias.astype(BF16), pad], axis=0)
        m_scr[h] = jnp.full((1, MOBA_BLOCK), MASK_NEG, F32)
        acc_scr[h] = jnp.zeros((HEAD_DIM + SUM_ROWS, MOBA_BLOCK), F32)

    def scores(h, blk):
        s_scr[h] = _dot(kx_ref[h, blk], qx_scr[h])

    def softmax(h, keep=None):
        for c in range(MOBA_BLOCK // LANES):
            lanes = slice(c * LANES, (c + 1) * LANES)
            s = s_scr[h, :, lanes]
            if keep is not None:
                s = jnp.where(keep[:, lanes], s, MASK_NEG)
            m_old = m_scr[h, :, lanes]
            m_new = jnp.maximum(m_old, jnp.max(s, axis=0, keepdims=True))
            m_scr[h, :, lanes] = m_new
            alpha_scr[h, :, lanes] = jnp.exp2(m_old - m_new)
            p_scr[h, :, lanes] = jnp.exp2(s - m_new).astype(BF16)

    def accumulate(h, blk):
        acc_scr[h] = alpha_scr[h] * acc_scr[h] + _dot(vT_ref[h, blk], p_scr[h])

    key_row = lax.broadcasted_iota(jnp.int32, (MOBA_BLOCK, MOBA_BLOCK), 0)
    qry_col = lax.broadcasted_iota(jnp.int32, (MOBA_BLOCK, MOBA_BLOCK), 1)
    causal = key_row <= qry_col
    for h in heads:
        scores(h, i)
    for h in heads:
        softmax(h, causal)
    for h in heads:
        scores(h, 0)

    def body(t, _):
        blk_prev = jnp.where(t == 1, i, t - 2)
        for h in heads:
            accumulate(h, blk_prev)
        for h in heads:
            softmax(h)
        for h in heads:
            scores(h, t)
        return 0

    lax.fori_loop(1, i + 1, body, 0)
    blk_last = jnp.maximum(i - 1, 0)
    for h in heads:
        accumulate(h, blk_last)
        acc = acc_scr[h]
        o_ref[h * HEAD_DIM:(h + 1) * HEAD_DIM, :] = acc[:HEAD_DIM] / acc[HEAD_DIM:HEAD_DIM + 1]


def _prompt_attn(qT, kmean, kx, vT, page_table, q_col, cache_kT):
    b, _, s = qT.shape
    nb = s // MOBA_BLOCK
    db, n_pages = page_table.shape
    steps_per_seq, rem = divmod(b * nb, db)
    assert rem == 0 and n_pages % (steps_per_seq * PAGES_PER_BLOCK * 8) == 0
    pages_per_step = n_pages // steps_per_seq
    blocks_per_step = pages_per_step // PAGES_PER_BLOCK
    once = pl.Buffered(1)
    seq_of = lambda bi, i: (bi * nb + i) // steps_per_seq
    grid_spec = pltpu.PrefetchScalarGridSpec(
        num_scalar_prefetch=1,
        grid=(b, nb),
        in_specs=[
            pl.BlockSpec((None, ATT_WIDTH, MOBA_BLOCK), lambda bi, i, pt: (bi, 0, i)),
            pl.BlockSpec((None, N_HEADS, nb, HEAD_DIM), lambda bi, i, pt: (bi, 0, 0, 0)),
            pl.BlockSpec((None, N_HEADS, nb, MOBA_BLOCK, LANES), lambda bi, i, pt: (bi, 0, 0, 0, 0),
                         pipeline_mode=once),
            pl.BlockSpec((None, N_HEADS, nb, HEAD_DIM + SUM_ROWS, MOBA_BLOCK), lambda bi, i, pt: (bi, 0, 0, 0, 0),
                         pipeline_mode=once),
            pl.BlockSpec((None, N_HEADS, HEAD_DIM, PAGE_SIZE), lambda bi, i, pt: (seq_of(bi, i), 0, 0, 0)),
            pl.BlockSpec(memory_space=pl.ANY),
        ],
        out_specs=[
            pl.BlockSpec((None, ATT_WIDTH, MOBA_BLOCK), lambda bi, i, pt: (bi, 0, i)),
            pl.BlockSpec((None, N_HEADS, blocks_per_step, MOBA_BLOCK),
                         lambda bi, i, pt: (seq_of(bi, i), 0, (bi * nb + i) % steps_per_seq, 0)),
        ],
        scratch_shapes=[
            pltpu.VMEM((N_HEADS, LANES, MOBA_BLOCK), BF16),
            pltpu.VMEM((N_HEADS, MOBA_BLOCK, MOBA_BLOCK), F32),
            pltpu.VMEM((N_HEADS, MOBA_BLOCK, MOBA_BLOCK), BF16),
            pltpu.VMEM((N_HEADS, HEAD_DIM + SUM_ROWS, MOBA_BLOCK), F32),
            pltpu.VMEM((N_HEADS, 1, MOBA_BLOCK), F32),
            pltpu.VMEM((N_HEADS, 1, MOBA_BLOCK), F32),
            pltpu.VMEM((pages_per_step, N_HEADS, HEAD_DIM, PAGE_SIZE), F32),
            pltpu.SemaphoreType.DMA,
        ],
    )
    return pl.pallas_call(
        _prompt_attn_kernel,
        grid_spec=grid_spec,
        out_shape=[jax.ShapeDtypeStruct((b, ATT_WIDTH, s), F32),
                   jax.ShapeDtypeStruct((db, N_HEADS, n_pages // PAGES_PER_BLOCK, MOBA_BLOCK), F32)],
        compiler_params=pltpu.CompilerParams(
            dimension_semantics=("arbitrary", "arbitrary"), vmem_limit_bytes=ATTN_VMEM_LIMIT),
        name="prompt_attn",
    )(page_table, qT, kmean, kx, vT, q_col, cache_kT)


def _pool_mix(diffs, wmix_ref, scale):
    mixed = [_dot(d.astype(BF16), wmix_ref[g]) for g, d in enumerate(diffs)]
    return jnp.concatenate(mixed, axis=1) * scale


def _layer_tail(x, att, zb, pool, za, ga, gb, p, wpo_ref, wao_ref, wo_ref, g_ref, b_ref, wple_ref, wgate_ref):
    y_pool = _dot((pool * _silu(za)).astype(BF16), wpo_ref[...])
    y_att = _dot((att * _silu(zb)).astype(BF16), wao_ref[...])
    merged = _sigmoid(ga) * y_pool + _sigmoid(gb) * y_att
    r = DEEPNORM_ALPHA * x + _dot(merged.astype(BF16), wo_ref[...])
    mu = jnp.mean(r, axis=-1, keepdims=True)
    c = r - mu
    var = jnp.mean(c * c, axis=-1, keepdims=True)
    h = c * lax.rsqrt(var + LN_EPS) * g_ref[...] + b_ref[...]
    gate = _sigmoid(_dot(h.astype(BF16), wgate_ref[...]))
    return h + gate * _dot(p.astype(BF16), wple_ref[...])


def _prompt_tail_kernel(x_ref, attT_ref, zb_ref, u_ref, uprev_ref, za_ref, ga_ref, gb_ref, p_ref,
                        wmix_ref, scale_ref, wpo_ref, wao_ref, wo_ref, g_ref, b_ref, wple_ref, wgate_ref,
                        y_ref):
    i = pl.program_id(1)
    u = u_ref[...]
    hist = jnp.where(i > 0, uprev_ref[...], 0.0)
    ext = jnp.concatenate([hist, u], axis=0)
    pos = i * TAIL_TILE + lax.broadcasted_iota(jnp.int32, (TAIL_TILE, POOL_GROUP), 0)
    diffs = []
    for g, w in enumerate(POOL_WINDOWS):
        e = ext[:, g * POOL_GROUP:(g + 1) * POOL_GROUP]
        step = 1
        while step < w:
            e = e + pltpu.roll(e, step, 0)
            step *= 2
        cnt = jnp.minimum(pos + 1, w).astype(F32)
        diffs.append(e[HIST_ROWS:] / cnt - u[:, g * POOL_GROUP:(g + 1) * POOL_GROUP])
    for r0 in range(0, TAIL_TILE, TAIL_PART):
        rows = slice(r0, r0 + TAIL_PART)
        pool = _pool_mix([d[rows] for d in diffs], wmix_ref, scale_ref[...])
        y_ref[rows, :] = _layer_tail(x_ref[rows, :], attT_ref[:, rows].T, zb_ref[rows, :], pool, za_ref[rows, :],
                                     ga_ref[rows, :], gb_ref[rows, :], p_ref[rows, :], wpo_ref, wao_ref, wo_ref,
                                     g_ref, b_ref, wple_ref, wgate_ref)


def _full(shape):
    return pl.BlockSpec(shape, lambda *_: (0,) * len(shape))


def _tail_weight_specs():
    return [
        _full((len(POOL_WINDOWS), POOL_GROUP, POOL_GROUP)), _full((1, POOL_WIDTH)),
        _full((POOL_WIDTH, D_MODEL)), _full((ATT_WIDTH, D_MODEL)), _full((D_MODEL, D_MODEL)),
        _full((1, D_MODEL)), _full((1, D_MODEL)), _full((PLE_DIM, D_MODEL)), _full((D_MODEL, D_MODEL)),
    ]


def _prompt_tail(x, attT, zb, u, za, ga, gb, p, weights):
    b, s, _ = x.shape
    assert s % TAIL_TILE == 0
    row = lambda width: pl.BlockSpec((None, TAIL_TILE, width), lambda bi, i: (bi, i, 0))
    hist_blocks = TAIL_TILE // HIST_ROWS
    in_specs = [
        row(D_MODEL),
        pl.BlockSpec((None, ATT_WIDTH, TAIL_TILE), lambda bi, i: (bi, 0, i)),
        row(ATT_WIDTH), row(POOL_WIDTH),
        pl.BlockSpec((None, HIST_ROWS, POOL_WIDTH), lambda bi, i: (bi, jnp.maximum(i * hist_blocks - 1, 0), 0)),
        row(POOL_WIDTH), row(D_MODEL), row(D_MODEL), row(PLE_DIM),
    ] + _tail_weight_specs()
    return pl.pallas_call(
        _prompt_tail_kernel,
        grid=(b, s // TAIL_TILE),
        in_specs=in_specs,
        out_specs=row(D_MODEL),
        out_shape=jax.ShapeDtypeStruct((b, s, D_MODEL), F32),
        compiler_params=pltpu.CompilerParams(
            dimension_semantics=("arbitrary", "arbitrary"), vmem_limit_bytes=VMEM_LIMIT),
        name="prompt_tail",
    )(x, attT, zb, u, u, za, ga, gb, p, *weights)


def _sample_inproj_kernel(x_ref, w_ref, cos_ref, sin_ref,
                          q_ref, k_ref, v_ref, zb_ref, u_ref, za_ref, ga_ref, gb_ref):
    xb = x_ref[...].astype(BF16)
    cos = cos_ref[...]
    sin = sin_ref[...]
    q_ref[...] = _rope(_project(xb, w_ref, 0), cos, sin) * Q_SCALE
    k_ref[...] = _rope(_project(xb, w_ref, 1), cos, sin)
    v_ref[...] = _project(xb, w_ref, 2)
    zb_ref[...] = _project(xb, w_ref, 3)
    u_ref[...] = _project(xb, w_ref, 4)
    za_ref[...] = _project(xb, w_ref, 5)
    ga_ref[...] = _project(xb, w_ref, 6)
    gb_ref[...] = _project(xb, w_ref, 7)


def _sample_inproj(x, w_in, cos, sin):
    n = x.shape[0]
    widths = (ATT_WIDTH, ATT_WIDTH, ATT_WIDTH, ATT_WIDTH, POOL_WIDTH, POOL_WIDTH, D_MODEL, D_MODEL)
    return pl.pallas_call(
        _sample_inproj_kernel,
        out_shape=[jax.ShapeDtypeStruct((n, w), F32) for w in widths],
        compiler_params=pltpu.CompilerParams(vmem_limit_bytes=VMEM_LIMIT),
        name="sample_inproj",
    )(x, w_in, cos, sin)


def _sample_select_kernel(logit_ref, q_ref, kn_ref, psel_ref, ids_ref, den_ref, pnew_ref):
    logits = logit_ref[...]
    n_blk = logits.shape[1]
    sc = jnp.sum(logits, axis=2, keepdims=True) / MOBA_BLOCK
    blk = lax.broadcasted_iota(jnp.int32, sc.shape, 1)
    picks = []
    for t in range(MOBA_TOPK):
        m = jnp.max(sc, axis=1, keepdims=True)
        first = jnp.min(jnp.where(sc == m, blk, n_blk), axis=1, keepdims=True)
        ids_ref[:, t:t + 1, :] = jnp.broadcast_to(first, (N_HEADS, 1, LANES))
        picks.append(blk == first)
        sc = jnp.where(picks[-1], -jnp.inf, sc)
    chosen = picks[0]
    for pick in picks[1:]:
        chosen = chosen | pick
    s_new = jnp.sum(q_ref[...] * kn_ref[...], axis=1, keepdims=True)[:, :, 0:1]
    masked = jnp.where(chosen, logits, MASK_NEG)
    m = jnp.max(jnp.max(masked, axis=2, keepdims=True), axis=1, keepdims=True)
    m = jnp.maximum(m, s_new)
    p = jnp.exp2(masked - m)
    p_new = jnp.exp2(s_new - m)
    den = jnp.sum(jnp.sum(p, axis=2, keepdims=True), axis=1, keepdims=True) + p_new
    den_ref[...] = jnp.broadcast_to(den, den_ref.shape)
    pnew_ref[...] = jnp.broadcast_to(p_new, pnew_ref.shape)
    for t, pick in enumerate(picks):
        psel_ref[:, t:t + 1, :] = jnp.sum(jnp.where(pick, p, 0.0), axis=1, keepdims=True)


def _sample_select(logits, q, k_new):
    db, _, n_blk, _ = logits.shape
    per_seq = lambda *tail: pl.BlockSpec((None, N_HEADS) + tail, lambda b: (b,) + (0,) * (len(tail) + 1))
    return pl.pallas_call(
        _sample_select_kernel,
        grid=(db,),
        in_specs=[per_seq(n_blk, MOBA_BLOCK), per_seq(HEAD_DIM, PAGE_SIZE), per_seq(HEAD_DIM, PAGE_SIZE)],
        out_specs=[per_seq(MOBA_TOPK, MOBA_BLOCK), per_seq(MOBA_TOPK, LANES), per_seq(1, LANES), per_seq(1, LANES)],
        out_shape=[jax.ShapeDtypeStruct((db, N_HEADS, MOBA_TOPK, MOBA_BLOCK), F32),
                   jax.ShapeDtypeStruct((db, N_HEADS, MOBA_TOPK, LANES), jnp.int32),
                   jax.ShapeDtypeStruct((db, N_HEADS, 1, LANES), F32),
                   jax.ShapeDtypeStruct((db, N_HEADS, 1, LANES), F32)],
        compiler_params=pltpu.CompilerParams(dimension_semantics=("arbitrary",), vmem_limit_bytes=VMEM_LIMIT),
        name="sample_select",
    )(logits, q, k_new)


PV_HEADS = 8


def _sample_pv_kernel(pt_ref, ids_ref, *refs):
    per_head = MOBA_TOPK * PAGES_PER_BLOCK
    slabs = refs[:PV_HEADS * per_head]
    psel_ref, den_ref, pnew_ref, vn_ref, o_ref = refs[PV_HEADS * per_head:]
    for hh in range(PV_HEADS):
        acc = pnew_ref[hh] * vn_ref[hh] * (1.0 / PAGE_SIZE)
        for t in range(MOBA_TOPK):
            for r in range(PAGES_PER_BLOCK):
                slab = slabs[(hh * MOBA_TOPK + t) * PAGES_PER_BLOCK + r][...]
                acc = acc + slab * psel_ref[hh, t:t + 1, r * PAGE_SIZE:(r + 1) * PAGE_SIZE]
        out = jnp.sum(acc, axis=1, keepdims=True) / den_ref[hh, :, 0:1]
        o_ref[hh] = jnp.broadcast_to(out, o_ref.shape[1:])


def _sample_pv(page_table, ids, cache_vT, psel, den, pnew, v_new):
    db = page_table.shape[0]

    def slab_spec(hh, t, r):
        def index(b, hg, pt, ids):
            h = hg * PV_HEADS + hh
            blk = jnp.clip(ids[(b * N_HEADS + h) * MOBA_TOPK + t], 0, pt.shape[1] // PAGES_PER_BLOCK - 1)
            return (0, pt[b, blk * PAGES_PER_BLOCK + r], h, 0, 0)
        return pl.BlockSpec((None, None, None, HEAD_DIM, PAGE_SIZE), index)

    slab_specs = [slab_spec(hh, t, r) for hh in range(PV_HEADS) for t in range(MOBA_TOPK)
                  for r in range(PAGES_PER_BLOCK)]
    group = lambda *tail: pl.BlockSpec((None, PV_HEADS) + tail, lambda b, hg, pt, ids: (b, hg) + (0,) * len(tail))
    grid_spec = pltpu.PrefetchScalarGridSpec(
        num_scalar_prefetch=2,
        grid=(db, N_HEADS // PV_HEADS),
        in_specs=slab_specs + [group(MOBA_TOPK, MOBA_BLOCK), group(1, LANES), group(1, LANES),
                               group(HEAD_DIM, PAGE_SIZE)],
        out_specs=group(HEAD_DIM, LANES),
    )
    return pl.pallas_call(
        _sample_pv_kernel,
        grid_spec=grid_spec,
        out_shape=jax.ShapeDtypeStruct((db, N_HEADS, HEAD_DIM, LANES), F32),
        compiler_params=pltpu.CompilerParams(
            dimension_semantics=("arbitrary", "arbitrary"), vmem_limit_bytes=VMEM_LIMIT),
        name="sample_pv",
    )(page_table, ids, *([cache_vT] * len(slab_specs)), psel, den, pnew, v_new)


def _sample_tail_kernel(x_ref, att_ref, zb_ref, u_ref, state_ref, za_ref, ga_ref, gb_ref, p_ref,
                        wmix_ref, scale_ref, wpo_ref, wao_ref, wo_ref, g_ref, b_ref, wple_ref, wgate_ref,
                        y_ref):
    u = u_ref[...]
    diffs = []
    for g, w in enumerate(POOL_WINDOWS):
        lanes = slice(g * POOL_GROUP, (g + 1) * POOL_GROUP)
        total = u[:, lanes]
        for back in range(1, w):
            total = total + state_ref[POOL_STATE - back, :, lanes]
        diffs.append(total / float(w) - u[:, lanes])
    pool = _pool_mix(diffs, wmix_ref, scale_ref[...])
    y_ref[...] = _layer_tail(x_ref[...], att_ref[...], zb_ref[...], pool, za_ref[...], ga_ref[...],
                             gb_ref[...], p_ref[...], wpo_ref, wao_ref, wo_ref, g_ref, b_ref,
                             wple_ref, wgate_ref)


def _sample_tail(x, att, zb, u, state_t, za, ga, gb, p, weights):
    return pl.pallas_call(
        _sample_tail_kernel,
        out_shape=jax.ShapeDtypeStruct(x.shape, F32),
        compiler_params=pltpu.CompilerParams(vmem_limit_bytes=VMEM_LIMIT),
        name="sample_tail",
    )(x, att, zb, u, state_t, za, ga, gb, p, *weights)


def _rope_tables(pos):
    inv = ROPE_THETA ** (-jnp.arange(HALF_DIM, dtype=F32) / HALF_DIM)
    ang = pos.astype(F32)[:, None] * inv[None, :]
    cos = jnp.cos(ang)
    sin = jnp.sin(ang)
    return jnp.tile(cos, (1, 4)), jnp.tile(jnp.concatenate([-sin, sin], axis=1), (1, 2))


def kernel(x_prompt, x_sample, cache_k, cache_v, state_pool, page_table, p_prompt, p_sample, w_in, w_pool_mix, pool_scale, w_pool_out, w_att_out, w_o, ln_g, ln_b, w_ple, w_ple_gate):
    assert w_in.shape[0] == DEPTH
    b, s, _ = x_prompt.shape
    db, n, _ = x_sample.shape
    assert n == 1
    n_pages = page_table.shape[1]
    past_len = n_pages * PAGE_SIZE
    assert past_len % MOBA_BLOCK == 0 and past_len // MOBA_BLOCK >= MOBA_TOPK
    nb = s // MOBA_BLOCK

    w_in_b = w_in[0].astype(BF16)
    weights = (w_pool_mix[0].astype(BF16), pool_scale[0][None, :], w_pool_out[0].astype(BF16),
               w_att_out[0].astype(BF16), w_o[0].astype(BF16), ln_g[0][None, :], ln_b[0][None, :],
               w_ple[0].astype(BF16), w_ple_gate[0].astype(BF16))

    cos_s, sin_s = _rope_tables(jnp.full((1,), past_len))
    q_s, k_s, v_s, zb_s, u_s, za_s, ga_s, gb_s = _sample_inproj(x_sample[:, 0], w_in_b, cos_s, sin_s)
    cache_kT = cache_k.transpose(0, 1, 3, 4, 2)
    cache_vT = cache_v.transpose(0, 1, 3, 4, 2)
    column = lambda t: jnp.broadcast_to(t.reshape(db, N_HEADS, HEAD_DIM, 1), (db, N_HEADS, HEAD_DIM, PAGE_SIZE))

    cos_p, sin_p = _rope_tables(jnp.arange(s))
    qT, kx, vT, k_p, v_p, kmean, zb, u_p, za, ga, gb = _prompt_inproj(x_prompt, w_in_b, cos_p, sin_p)
    kmean_h = kmean.reshape(b, nb, N_HEADS, HEAD_DIM).transpose(0, 2, 1, 3)
    attT, logits_s = _prompt_attn(qT, kmean_h, kx, vT, page_table, column(q_s), cache_kT)
    y_prompt = _prompt_tail(x_prompt, attT, zb, u_p, za, ga, gb, p_prompt[0], weights)

    psel, ids, den, pnew = _sample_select(logits_s, column(q_s), column(k_s))
    att_s = _sample_pv(page_table, ids[..., 0].reshape(-1), cache_vT, psel, den, pnew, column(v_s))[..., 0]
    state = state_pool[0]
    y_sample = _sample_tail(x_sample[:, 0], att_s.reshape(db, ATT_WIDTH), zb_s, u_s, state.transpose(1, 0, 2),
                            za_s, ga_s, gb_s, p_sample[0][:, 0], weights)

    heads = (N_HEADS, HEAD_DIM)
    pool_prompt = u_p[:, s - POOL_STATE:]
    pool_sample = jnp.concatenate([state[:, 1:], u_s[:, None, :]], axis=1)
    return (y_prompt, y_sample[:, None, :],
            k_p.reshape(1, b, s, *heads), v_p.reshape(1, b, s, *heads), pool_prompt[None],
            k_s.reshape(1, db, 1, *heads), v_s.reshape(1, db, 1, *heads), pool_sample[None])
```

```python
import functools

import numpy as np
import jax
import jax.numpy as jnp
from jax import lax
from jax.experimental import pallas as pl
from jax.experimental.pallas import tpu as pltpu

D_MODEL = 1024
N_HEADS = 8
HEAD_DIM = 64
HALF_DIM = HEAD_DIM // 2
ATT_WIDTH = N_HEADS * HEAD_DIM
POOL_WINDOWS = (2, 4, 8, 16)
POOL_WIDTH = D_MODEL // 2
POOL_GROUP = POOL_WIDTH // len(POOL_WINDOWS)
POOL_STATE = max(POOL_WINDOWS) - 1
MOBA_BLOCK = 256
MOBA_TOPK = 3
PAGE_SIZE = 128
PAGES_PER_BLOCK = MOBA_BLOCK // PAGE_SIZE
ROPE_THETA = 10000.0
PLE_DIM = 256
LN_EPS = 1e-5
DEPTH = 1
DEEPNORM_ALPHA = (2 * DEPTH) ** 0.25
IN_SPLITS = (ATT_WIDTH, ATT_WIDTH, ATT_WIDTH, ATT_WIDTH, POOL_WIDTH, POOL_WIDTH, D_MODEL, D_MODEL)
IN_OFFS = tuple(int(v) for v in np.cumsum((0,) + IN_SPLITS))
IN_WIDTH = IN_OFFS[-1]
Q_SCALE = HEAD_DIM ** -0.5 * float(np.log2(np.e))
SUM_ROWS = 16

LANES = 128
HIST_ROWS = 16
MASK_NEG = -1e30
ROW_TILE = 256
TAIL_TILE = 512
TAIL_PART = 256
VMEM_LIMIT = 52 * 1024 * 1024
ATTN_VMEM_LIMIT = 57 * 1024 * 1024

F32 = jnp.float32
BF16 = jnp.bfloat16


def _sigmoid(x):
    return 0.5 * jnp.tanh(0.5 * x) + 0.5


def _silu(x):
    return x * _sigmoid(x)


def _dot(a, b):
    return jnp.dot(a, b, preferred_element_type=F32)


def _rope(t, cos, sin_signed):
    lane = lax.broadcasted_iota(jnp.int32, (1, LANES), 1)
    first_half = (lane % HEAD_DIM) < HALF_DIM
    outs = []
    for c in range(t.shape[1] // LANES):
        tc = t[:, c * LANES:(c + 1) * LANES]
        partner = jnp.where(first_half, pltpu.roll(tc, LANES - HALF_DIM, 1), pltpu.roll(tc, HALF_DIM, 1))
        outs.append(tc * cos + partner * sin_signed)
    return jnp.concatenate(outs, axis=1)


def _project(xb, w_ref, seg):
    return _dot(xb, w_ref[:, IN_OFFS[seg]:IN_OFFS[seg + 1]])


def _prompt_inproj_kernel(x_ref, w_ref, cos_ref, sin_ref,
                          qT_ref, kx_ref, vT_ref, k_ref, v_ref, km_ref,
                          zb_ref, u_ref, za_ref, ga_ref, gb_ref):
    i = pl.program_id(1)
    xb = x_ref[...].astype(BF16)
    cos = cos_ref[...]
    sin = sin_ref[...]

    q = _rope(_project(xb, w_ref, 0), cos, sin) * Q_SCALE
    qT_ref[...] = q.T.astype(BF16)

    k = _rope(_project(xb, w_ref, 1), cos, sin)
    k_ref[...] = k
    km_ref[...] = jnp.sum(k, axis=0, keepdims=True) / MOBA_BLOCK
    lane = lax.broadcasted_iota(jnp.int32, (ROW_TILE, LANES), 1)
    extra = jnp.where(lane - HEAD_DIM == i, 1.0, 0.0)
    for h in range(N_HEADS):
        pair = k[:, (h // 2) * LANES:(h // 2 + 1) * LANES]
        base = pair if h % 2 == 0 else pltpu.roll(pair, HEAD_DIM, 1)
        kx_ref[h, 0] = jnp.where(lane < HEAD_DIM, base, extra).astype(BF16)

    v = _project(xb, w_ref, 2)
    v_ref[...] = v
    vT = v.T.astype(BF16).reshape(N_HEADS, HEAD_DIM, ROW_TILE)
    vT_ref[:, 0] = jnp.concatenate([vT, jnp.ones((N_HEADS, SUM_ROWS, ROW_TILE), BF16)], axis=1)

    zb_ref[...] = _project(xb, w_ref, 3)
    u_ref[...] = _project(xb, w_ref, 4)
    za_ref[...] = _project(xb, w_ref, 5)
    ga_ref[...] = _project(xb, w_ref, 6)
    gb_ref[...] = _project(xb, w_ref, 7)


def _prompt_inproj(x, w_in, cos, sin):
    b, s, _ = x.shape
    nb = s // MOBA_BLOCK
    assert ROW_TILE == MOBA_BLOCK and s % ROW_TILE == 0 and nb <= 32
    row = lambda width: pl.BlockSpec((None, ROW_TILE, width), lambda bi, i: (bi, i, 0))
    out_shape = [
        jax.ShapeDtypeStruct((b, ATT_WIDTH, s), BF16),
        jax.ShapeDtypeStruct((b, N_HEADS, nb, MOBA_BLOCK, LANES), BF16),
        jax.ShapeDtypeStruct((b, N_HEADS, nb, HEAD_DIM + SUM_ROWS, MOBA_BLOCK), BF16),
        jax.ShapeDtypeStruct((b, s, ATT_WIDTH), F32),
        jax.ShapeDtypeStruct((b, s, ATT_WIDTH), F32),
        jax.ShapeDtypeStruct((b, nb, 1, ATT_WIDTH), F32),
        jax.ShapeDtypeStruct((b, s, ATT_WIDTH), F32),
        jax.ShapeDtypeStruct((b, s, POOL_WIDTH), F32),
        jax.ShapeDtypeStruct((b, s, POOL_WIDTH), F32),
        jax.ShapeDtypeStruct((b, s, D_MODEL), F32),
        jax.ShapeDtypeStruct((b, s, D_MODEL), F32),
    ]
    out_specs = [
        pl.BlockSpec((None, ATT_WIDTH, ROW_TILE), lambda bi, i: (bi, 0, i)),
        pl.BlockSpec((None, N_HEADS, 1, MOBA_BLOCK, LANES), lambda bi, i: (bi, 0, i, 0, 0)),
        pl.BlockSpec((None, N_HEADS, 1, HEAD_DIM + SUM_ROWS, MOBA_BLOCK), lambda bi, i: (bi, 0, i, 0, 0)),
        row(ATT_WIDTH), row(ATT_WIDTH),
        pl.BlockSpec((None, None, 1, ATT_WIDTH), lambda bi, i: (bi, i, 0, 0)),
        row(ATT_WIDTH), row(POOL_WIDTH), row(POOL_WIDTH), row(D_MODEL), row(D_MODEL),
    ]
    in_specs = [
        row(D_MODEL),
        pl.BlockSpec((D_MODEL, IN_WIDTH), lambda bi, i: (0, 0)),
        pl.BlockSpec((ROW_TILE, LANES), lambda bi, i: (i, 0)),
        pl.BlockSpec((ROW_TILE, LANES), lambda bi, i: (i, 0)),
    ]
    return pl.pallas_call(
        _prompt_inproj_kernel,
        grid=(b, s // ROW_TILE),
        in_specs=in_specs, out_specs=out_specs, out_shape=out_shape,
        compiler_params=pltpu.CompilerParams(
            dimension_semantics=("arbitrary", "arbitrary"), vmem_limit_bytes=VMEM_LIMIT),
        name="prompt_inproj",
    )(x, w_in, cos, sin)


def _select_bias(scores, n_past, own):
    blk = lax.broadcasted_iota(jnp.int32, scores.shape, 0)
    n_blk = scores.shape[0]
    s = jnp.where(blk < n_past, scores, -jnp.inf)
    chosen = blk == own
    for _ in range(MOBA_TOPK):
        m = jnp.max(s, axis=0, keepdims=True)
        cand = jnp.where((s == m) & (m > -jnp.inf), blk, n_blk)
        first = jnp.min(cand, axis=0, keepdims=True)
        pick = blk == first
        chosen = chosen | pick
        s = jnp.where(pick, -jnp.inf, s)
    return jnp.where(chosen, 0.0, MASK_NEG)


def _sample_logits_stage(pt_ref, qcol_ref, pool_ref, logit_ref, page_buf, page_sem):
    n_steps = pl.num_programs(0) * pl.num_programs(1)
    g = pl.program_id(0) * pl.num_programs(1) + pl.program_id(1)
    pages_per_step = page_buf.shape[0]
    steps_per_seq = pt_ref.shape[1] // pages_per_step

    def page_copy(step, r):
        seq = step // steps_per_seq
        first = (step % steps_per_seq) * pages_per_step
        return pltpu.make_async_copy(pool_ref.at[0, pt_ref[seq, first + r]], page_buf.at[r], page_sem)

    @pl.when(g == 0)
    def _():
        for r in range(pages_per_step):
            page_copy(0, r).start()

    for r in range(pages_per_step):
        page_copy(g, r).wait()

    rows_per_store = 8
    for h in range(N_HEADS):
        qh = qcol_ref[h]
        for first_blk in range(0, pages_per_step // PAGES_PER_BLOCK, rows_per_store):
            rows = []
            for blk in range(first_blk, first_blk + rows_per_store):
                halves = [jnp.sum(page_buf[blk * PAGES_PER_BLOCK + r, h] * qh, axis=0, keepdims=True)
                          for r in range(PAGES_PER_BLOCK)]
                rows.append(jnp.concatenate(halves, axis=1))
            logit_ref[h, first_blk:first_blk + rows_per_store, :] = jnp.concatenate(rows, axis=0)

    @pl.when(g + 1 < n_steps)
    def _():
        for r in range(pages_per_step):
            page_copy(g + 1, r).start()


def _prompt_attn_kernel(pt_ref, qT_ref, km_ref, kx_ref, vT_ref, qcol_ref, pool_ref, o_ref, logit_ref,
                        qx_scr, s_scr, p_scr, acc_scr, m_scr, alpha_scr, page_buf, page_sem):
    _sample_logits_stage(pt_ref, qcol_ref, pool_ref, logit_ref, page_buf, page_sem)

    i = pl.program_id(1)
    n_blk = km_ref.shape[1]
    heads = range(N_HEADS)

    pad = jnp.zeros((LANES - HEAD_DIM - n_blk, MOBA_BLOCK), BF16)
    for h in heads:
        qT = qT_ref[h * HEAD_DIM:(h + 1) * HEAD_DIM, :]
        bias = _select_bias(_dot(km_ref[h].astype(BF16), qT), i, i)
        qx_scr[h] = jnp.concatenate([qT, bias.astype(BF16), pad], axis=0)
        m_scr[h] = jnp.full((1, MOBA_BLOCK), MASK_NEG, F32)
        acc_scr[h] = jnp.zeros((HEAD_DIM + SUM_ROWS, MOBA_BLOCK), F32)

    def scores(h, blk, st):
        s_scr[st, h] = _dot(kx_ref[h, blk], qx_scr[h])

    def softmax(h, st, keep=None):
        for c in range(MOBA_BLOCK // LANES):
            lanes = slice(c * LANES, (c + 1) * LANES)
            s = s_scr[st, h, :, lanes]
            if keep is not None:
                s = jnp.where(keep[:, lanes], s, MASK_NEG)
            m_old = m_scr[h, :, lanes]
            m_new = jnp.maximum(m_old, jnp.max(s, axis=0, keepdims=True))
            m_scr[h, :, lanes] = m_new
            alpha_scr[h, :, lanes] = jnp.exp2(m_old - m_new)
            p_scr[h, :, lanes] = jnp.exp2(s - m_new).astype(BF16)

    def accumulate(h, blk):
        tiles = [slice(c * LANES, (c + 1) * LANES) for c in range(MOBA_BLOCK // LANES)]
        p = jnp.concatenate([p_scr[h, :, lanes] for lanes in tiles], axis=1)
        alpha = jnp.concatenate([alpha_scr[h, :, lanes] for lanes in tiles], axis=1)
        acc_scr[h] = alpha * acc_scr[h] + _dot(vT_ref[h, blk], p)

    key_row = lax.broadcasted_iota(jnp.int32, (MOBA_BLOCK, MOBA_BLOCK), 0)
    qry_col = lax.broadcasted_iota(jnp.int32, (MOBA_BLOCK, MOBA_BLOCK), 1)
    causal = key_row <= qry_col
    for h in heads:
        scores(h, i, 0)
    for h in heads:
        softmax(h, 0, causal)
    for h in heads:
        scores(h, 0, 1)

    def half_trip(scores_blk, accumulate_blk, st):
        if scores_blk is not None:
            for h in heads:
                scores(h, scores_blk, 1 - st)
        for h in heads:
            accumulate(h, accumulate_blk)
        for h in heads:
            softmax(h, st)

    def pair(k, _):
        t = 2 * k + 1
        half_trip(t, jnp.where(k == 0, i, t - 2), 1)
        half_trip(t + 1, t - 1, 0)
        return 0

    lax.fori_loop(0, i // 2, pair, 0)

    def finish(blk):
        for h in heads:
            accumulate(h, blk)
            acc = acc_scr[h]
            o_ref[h * HEAD_DIM:(h + 1) * HEAD_DIM, :] = acc[:HEAD_DIM] / acc[HEAD_DIM:HEAD_DIM + 1]

    @pl.when(i % 2 == 1)
    def _():
        half_trip(None, jnp.where(i == 1, i, i - 2), 1)
        finish(i - 1)

    @pl.when(i % 2 == 0)
    def _():
        finish(jnp.maximum(i - 1, 0))


def _prompt_attn(qT, kmean, kx, vT, page_table, q_col, cache_kT):
    b, _, s = qT.shape
    nb = s // MOBA_BLOCK
    db, n_pages = page_table.shape
    steps_per_seq, rem = divmod(b * nb, db)
    assert rem == 0 and n_pages % (steps_per_seq * PAGES_PER_BLOCK * 8) == 0
    pages_per_step = n_pages // steps_per_seq
    blocks_per_step = pages_per_step // PAGES_PER_BLOCK
    once = pl.Buffered(1)
    seq_of = lambda bi, i: (bi * nb + i) // steps_per_seq
    grid_spec = pltpu.PrefetchScalarGridSpec(
        num_scalar_prefetch=1,
        grid=(b, nb),
        in_specs=[
            pl.BlockSpec((None, ATT_WIDTH, MOBA_BLOCK), lambda bi, i, pt: (bi, 0, i)),
            pl.BlockSpec((None, N_HEADS, nb, HEAD_DIM), lambda bi, i, pt: (bi, 0, 0, 0)),
            pl.BlockSpec((None, N_HEADS, nb, MOBA_BLOCK, LANES), lambda bi, i, pt: (bi, 0, 0, 0, 0),
                         pipeline_mode=once),
            pl.BlockSpec((None, N_HEADS, nb, HEAD_DIM + SUM_ROWS, MOBA_BLOCK), lambda bi, i, pt: (bi, 0, 0, 0, 0),
                         pipeline_mode=once),
            pl.BlockSpec((None, N_HEADS, HEAD_DIM, PAGE_SIZE), lambda bi, i, pt: (seq_of(bi, i), 0, 0, 0)),
            pl.BlockSpec(memory_space=pl.ANY),
        ],
        out_specs=[
            pl.BlockSpec((None, ATT_WIDTH, MOBA_BLOCK), lambda bi, i, pt: (bi, 0, i)),
            pl.BlockSpec((None, N_HEADS, blocks_per_step, MOBA_BLOCK),
                         lambda bi, i, pt: (seq_of(bi, i), 0, (bi * nb + i) % steps_per_seq, 0)),
        ],
        scratch_shapes=[
            pltpu.VMEM((N_HEADS, LANES, MOBA_BLOCK), BF16),
            pltpu.VMEM((2, N_HEADS, MOBA_BLOCK, MOBA_BLOCK), F32),
            pltpu.VMEM((N_HEADS, MOBA_BLOCK, MOBA_BLOCK), BF16),
            pltpu.VMEM((N_HEADS, HEAD_DIM + SUM_ROWS, MOBA_BLOCK), F32),
            pltpu.VMEM((N_HEADS, 1, MOBA_BLOCK), F32),
            pltpu.VMEM((N_HEADS, 1, MOBA_BLOCK), F32),
            pltpu.VMEM((pages_per_step, N_HEADS, HEAD_DIM, PAGE_SIZE), F32),
            pltpu.SemaphoreType.DMA,
        ],
    )
    return pl.pallas_call(
        _prompt_attn_kernel,
        grid_spec=grid_spec,
        out_shape=[jax.ShapeDtypeStruct((b, ATT_WIDTH, s), F32),
                   jax.ShapeDtypeStruct((db, N_HEADS, n_pages // PAGES_PER_BLOCK, MOBA_BLOCK), F32)],
        compiler_params=pltpu.CompilerParams(
            dimension_semantics=("arbitrary", "arbitrary"), vmem_limit_bytes=ATTN_VMEM_LIMIT),
        name="prompt_attn",
    )(page_table, qT, kmean, kx, vT, q_col, cache_kT)


def _pool_mix(diffs, wmix_ref, scale):
    mixed = [_dot(d.astype(BF16), wmix_ref[g]) for g, d in enumerate(diffs)]
    return jnp.concatenate(mixed, axis=1) * scale


def _layer_tail(x, att, zb, pool, za, ga, gb, p, wpo_ref, wao_ref, wo_ref, g_ref, b_ref, wple_ref, wgate_ref):
    y_pool = _dot((pool * _silu(za)).astype(BF16), wpo_ref[...])
    y_att = _dot((att * _silu(zb)).astype(BF16), wao_ref[...])
    merged = _sigmoid(ga) * y_pool + _sigmoid(gb) * y_att
    r = DEEPNORM_ALPHA * x + _dot(merged.astype(BF16), wo_ref[...])
    mu = jnp.mean(r, axis=-1, keepdims=True)
    c = r - mu
    var = jnp.mean(c * c, axis=-1, keepdims=True)
    h = c * lax.rsqrt(var + LN_EPS) * g_ref[...] + b_ref[...]
    gate = _sigmoid(_dot(h.astype(BF16), wgate_ref[...]))
    return h + gate * _dot(p.astype(BF16), wple_ref[...])


def _prompt_tail_kernel(x_ref, attT_ref, zb_ref, u_ref, uprev_ref, za_ref, ga_ref, gb_ref, p_ref,
                        wmix_ref, scale_ref, wpo_ref, wao_ref, wo_ref, g_ref, b_ref, wple_ref, wgate_ref,
                        y_ref):
    i = pl.program_id(1)
    u = u_ref[...]
    hist = jnp.where(i > 0, uprev_ref[...], 0.0)
    ext = jnp.concatenate([hist, u], axis=0)
    pos = i * TAIL_TILE + lax.broadcasted_iota(jnp.int32, (TAIL_TILE, POOL_GROUP), 0)
    diffs = []
    for g, w in enumerate(POOL_WINDOWS):
        e = ext[:, g * POOL_GROUP:(g + 1) * POOL_GROUP]
        step = 1
        while step < w:
            e = e + pltpu.roll(e, step, 0)
            step *= 2
        cnt = jnp.minimum(pos + 1, w).astype(F32)
        diffs.append(e[HIST_ROWS:] / cnt - u[:, g * POOL_GROUP:(g + 1) * POOL_GROUP])
    for r0 in range(0, TAIL_TILE, TAIL_PART):
        rows = slice(r0, r0 + TAIL_PART)
        pool = _pool_mix([d[rows] for d in diffs], wmix_ref, scale_ref[...])
        y_ref[rows, :] = _layer_tail(x_ref[rows, :], attT_ref[:, rows].T, zb_ref[rows, :], pool, za_ref[rows, :],
                                     ga_ref[rows, :], gb_ref[rows, :], p_ref[rows, :], wpo_ref, wao_ref, wo_ref,
                                     g_ref, b_ref, wple_ref, wgate_ref)


def _full(shape):
    return pl.BlockSpec(shape, lambda *_: (0,) * len(shape))


def _tail_weight_specs():
    return [
        _full((len(POOL_WINDOWS), POOL_GROUP, POOL_GROUP)), _full((1, POOL_WIDTH)),
        _full((POOL_WIDTH, D_MODEL)), _full((ATT_WIDTH, D_MODEL)), _full((D_MODEL, D_MODEL)),
        _full((1, D_MODEL)), _full((1, D_MODEL)), _full((PLE_DIM, D_MODEL)), _full((D_MODEL, D_MODEL)),
    ]


def _prompt_tail(x, attT, zb, u, za, ga, gb, p, weights):
    b, s, _ = x.shape
    assert s % TAIL_TILE == 0
    row = lambda width: pl.BlockSpec((None, TAIL_TILE, width), lambda bi, i: (bi, i, 0))
    hist_blocks = TAIL_TILE // HIST_ROWS
    in_specs = [
        row(D_MODEL),
        pl.BlockSpec((None, ATT_WIDTH, TAIL_TILE), lambda bi, i: (bi, 0, i)),
        row(ATT_WIDTH), row(POOL_WIDTH),
        pl.BlockSpec((None, HIST_ROWS, POOL_WIDTH), lambda bi, i: (bi, jnp.maximum(i * hist_blocks - 1, 0), 0)),
        row(POOL_WIDTH), row(D_MODEL), row(D_MODEL), row(PLE_DIM),
    ] + _tail_weight_specs()
    return pl.pallas_call(
        _prompt_tail_kernel,
        grid=(b, s // TAIL_TILE),
        in_specs=in_specs,
        out_specs=row(D_MODEL),
        out_shape=jax.ShapeDtypeStruct((b, s, D_MODEL), F32),
        compiler_params=pltpu.CompilerParams(
            dimension_semantics=("arbitrary", "arbitrary"), vmem_limit_bytes=VMEM_LIMIT),
        name="prompt_tail",
    )(x, attT, zb, u, u, za, ga, gb, p, *weights)


def _sample_inproj_kernel(x_ref, w_ref, cos_ref, sin_ref,
                          q_ref, k_ref, v_ref, zb_ref, u_ref, za_ref, ga_ref, gb_ref):
    xb = x_ref[...].astype(BF16)
    cos = cos_ref[...]
    sin = sin_ref[...]
    q_ref[...] = _rope(_project(xb, w_ref, 0), cos, sin) * Q_SCALE
    k_ref[...] = _rope(_project(xb, w_ref, 1), cos, sin)
    v_ref[...] = _project(xb, w_ref, 2)
    zb_ref[...] = _project(xb, w_ref, 3)
    u_ref[...] = _project(xb, w_ref, 4)
    za_ref[...] = _project(xb, w_ref, 5)
    ga_ref[...] = _project(xb, w_ref, 6)
    gb_ref[...] = _project(xb, w_ref, 7)


def _sample_inproj(x, w_in, cos, sin):
    n = x.shape[0]
    widths = (ATT_WIDTH, ATT_WIDTH, ATT_WIDTH, ATT_WIDTH, POOL_WIDTH, POOL_WIDTH, D_MODEL, D_MODEL)
    return pl.pallas_call(
        _sample_inproj_kernel,
        out_shape=[jax.ShapeDtypeStruct((n, w), F32) for w in widths],
        compiler_params=pltpu.CompilerParams(vmem_limit_bytes=VMEM_LIMIT),
        name="sample_inproj",
    )(x, w_in, cos, sin)


def _sample_select_kernel(logit_ref, q_ref, kn_ref, psel_ref, ids_ref, den_ref, pnew_ref):
    logits = logit_ref[...]
    n_blk = logits.shape[1]
    sc = jnp.sum(logits, axis=2, keepdims=True) / MOBA_BLOCK
    blk = lax.broadcasted_iota(jnp.int32, sc.shape, 1)
    picks = []
    for t in range(MOBA_TOPK):
        m = jnp.max(sc, axis=1, keepdims=True)
        first = jnp.min(jnp.where(sc == m, blk, n_blk), axis=1, keepdims=True)
        ids_ref[:, t:t + 1, :] = jnp.broadcast_to(first, (N_HEADS, 1, LANES))
        picks.append(blk == first)
        sc = jnp.where(picks[-1], -jnp.inf, sc)
    chosen = picks[0]
    for pick in picks[1:]:
        chosen = chosen | pick
    s_new = jnp.sum(q_ref[...] * kn_ref[...], axis=1, keepdims=True)[:, :, 0:1]
    masked = jnp.where(chosen, logits, MASK_NEG)
    m = jnp.max(jnp.max(masked, axis=2, keepdims=True), axis=1, keepdims=True)
    m = jnp.maximum(m, s_new)
    p = jnp.exp2(masked - m)
    p_new = jnp.exp2(s_new - m)
    den = jnp.sum(jnp.sum(p, axis=2, keepdims=True), axis=1, keepdims=True) + p_new
    den_ref[...] = jnp.broadcast_to(den, den_ref.shape)
    pnew_ref[...] = jnp.broadcast_to(p_new, pnew_ref.shape)
    for t, pick in enumerate(picks):
        psel_ref[:, t:t + 1, :] = jnp.sum(jnp.where(pick, p, 0.0), axis=1, keepdims=True)


def _sample_select(logits, q, k_new):
    db, _, n_blk, _ = logits.shape
    per_seq = lambda *tail: pl.BlockSpec((None, N_HEADS) + tail, lambda b: (b,) + (0,) * (len(tail) + 1))
    return pl.pallas_call(
        _sample_select_kernel,
        grid=(db,),
        in_specs=[per_seq(n_blk, MOBA_BLOCK), per_seq(HEAD_DIM, PAGE_SIZE), per_seq(HEAD_DIM, PAGE_SIZE)],
        out_specs=[per_seq(MOBA_TOPK, MOBA_BLOCK), per_seq(MOBA_TOPK, LANES), per_seq(1, LANES), per_seq(1, LANES)],
        out_shape=[jax.ShapeDtypeStruct((db, N_HEADS, MOBA_TOPK, MOBA_BLOCK), F32),
                   jax.ShapeDtypeStruct((db, N_HEADS, MOBA_TOPK, LANES), jnp.int32),
                   jax.ShapeDtypeStruct((db, N_HEADS, 1, LANES), F32),
                   jax.ShapeDtypeStruct((db, N_HEADS, 1, LANES), F32)],
        compiler_params=pltpu.CompilerParams(dimension_semantics=("arbitrary",), vmem_limit_bytes=VMEM_LIMIT),
        name="sample_select",
    )(logits, q, k_new)


PV_HEADS = 8


def _sample_pv_kernel(pt_ref, ids_ref, *refs):
    per_head = MOBA_TOPK * PAGES_PER_BLOCK
    slabs = refs[:PV_HEADS * per_head]
    psel_ref, den_ref, pnew_ref, vn_ref, o_ref = refs[PV_HEADS * per_head:]
    for hh in range(PV_HEADS):
        acc = pnew_ref[hh] * vn_ref[hh] * (1.0 / PAGE_SIZE)
        for t in range(MOBA_TOPK):
            for r in range(PAGES_PER_BLOCK):
                slab = slabs[(hh * MOBA_TOPK + t) * PAGES_PER_BLOCK + r][...]
                acc = acc + slab * psel_ref[hh, t:t + 1, r * PAGE_SIZE:(r + 1) * PAGE_SIZE]
        out = jnp.sum(acc, axis=1, keepdims=True) / den_ref[hh, :, 0:1]
        o_ref[hh] = jnp.broadcast_to(out, o_ref.shape[1:])


def _sample_pv(page_table, ids, cache_vT, psel, den, pnew, v_new):
    db = page_table.shape[0]

    def slab_spec(hh, t, r):
        def index(b, hg, pt, ids):
            h = hg * PV_HEADS + hh
            blk = jnp.clip(ids[(b * N_HEADS + h) * MOBA_TOPK + t], 0, pt.shape[1] // PAGES_PER_BLOCK - 1)
            return (0, pt[b, blk * PAGES_PER_BLOCK + r], h, 0, 0)
        return pl.BlockSpec((None, None, None, HEAD_DIM, PAGE_SIZE), index)

    slab_specs = [slab_spec(hh, t, r) for hh in range(PV_HEADS) for t in range(MOBA_TOPK)
                  for r in range(PAGES_PER_BLOCK)]
    group = lambda *tail: pl.BlockSpec((None, PV_HEADS) + tail, lambda b, hg, pt, ids: (b, hg) + (0,) * len(tail))
    grid_spec = pltpu.PrefetchScalarGridSpec(
        num_scalar_prefetch=2,
        grid=(db, N_HEADS // PV_HEADS),
        in_specs=slab_specs + [group(MOBA_TOPK, MOBA_BLOCK), group(1, LANES), group(1, LANES),
                               group(HEAD_DIM, PAGE_SIZE)],
        out_specs=group(HEAD_DIM, LANES),
    )
    return pl.pallas_call(
        _sample_pv_kernel,
        grid_spec=grid_spec,
        out_shape=jax.ShapeDtypeStruct((db, N_HEADS, HEAD_DIM, LANES), F32),
        compiler_params=pltpu.CompilerParams(
            dimension_semantics=("arbitrary", "arbitrary"), vmem_limit_bytes=VMEM_LIMIT),
        name="sample_pv",
    )(page_table, ids, *([cache_vT] * len(slab_specs)), psel, den, pnew, v_new)


def _sample_tail_kernel(x_ref, att_ref, zb_ref, u_ref, state_ref, za_ref, ga_ref, gb_ref, p_ref,
                        wmix_ref, scale_ref, wpo_ref, wao_ref, wo_ref, g_ref, b_ref, wple_ref, wgate_ref,
                        y_ref):
    u = u_ref[...]
    diffs = []
    for g, w in enumerate(POOL_WINDOWS):
        lanes = slice(g * POOL_GROUP, (g + 1) * POOL_GROUP)
        total = u[:, lanes]
        for back in range(1, w):
            total = total + state_ref[POOL_STATE - back, :, lanes]
        diffs.append(total / float(w) - u[:, lanes])
    pool = _pool_mix(diffs, wmix_ref, scale_ref[...])
    y_ref[...] = _layer_tail(x_ref[...], att_ref[...], zb_ref[...], pool, za_ref[...], ga_ref[...],
                             gb_ref[...], p_ref[...], wpo_ref, wao_ref, wo_ref, g_ref, b_ref,
                             wple_ref, wgate_ref)


def _sample_tail(x, att, zb, u, state_t, za, ga, gb, p, weights):
    return pl.pallas_call(
        _sample_tail_kernel,
        out_shape=jax.ShapeDtypeStruct(x.shape, F32),
        compiler_params=pltpu.CompilerParams(vmem_limit_bytes=VMEM_LIMIT),
        name="sample_tail",
    )(x, att, zb, u, state_t, za, ga, gb, p, *weights)


def _rope_tables(pos):
    inv = ROPE_THETA ** (-jnp.arange(HALF_DIM, dtype=F32) / HALF_DIM)
    ang = pos.astype(F32)[:, None] * inv[None, :]
    cos = jnp.cos(ang)
    sin = jnp.sin(ang)
    return jnp.tile(cos, (1, 4)), jnp.tile(jnp.concatenate([-sin, sin], axis=1), (1, 2))


def kernel(x_prompt, x_sample, cache_k, cache_v, state_pool, page_table, p_prompt, p_sample, w_in, w_pool_mix, pool_scale, w_pool_out, w_att_out, w_o, ln_g, ln_b, w_ple, w_ple_gate):
    assert w_in.shape[0] == DEPTH
    b, s, _ = x_prompt.shape
    db, n, _ = x_sample.shape
    assert n == 1
    n_pages = page_table.shape[1]
    past_len = n_pages * PAGE_SIZE
    assert past_len % MOBA_BLOCK == 0 and past_len // MOBA_BLOCK >= MOBA_TOPK
    nb = s // MOBA_BLOCK

    w_in_b = w_in[0].astype(BF16)
    weights = (w_pool_mix[0].astype(BF16), pool_scale[0][None, :], w_pool_out[0].astype(BF16),
               w_att_out[0].astype(BF16), w_o[0].astype(BF16), ln_g[0][None, :], ln_b[0][None, :],
               w_ple[0].astype(BF16), w_ple_gate[0].astype(BF16))

    cos_s, sin_s = _rope_tables(jnp.full((1,), past_len))
    q_s, k_s, v_s, zb_s, u_s, za_s, ga_s, gb_s = _sample_inproj(x_sample[:, 0], w_in_b, cos_s, sin_s)
    cache_kT = cache_k.transpose(0, 1, 3, 4, 2)
    cache_vT = cache_v.transpose(0, 1, 3, 4, 2)
    column = lambda t: jnp.broadcast_to(t.reshape(db, N_HEADS, HEAD_DIM, 1), (db, N_HEADS, HEAD_DIM, PAGE_SIZE))

    cos_p, sin_p = _rope_tables(jnp.arange(s))
    qT, kx, vT, k_p, v_p, kmean, zb, u_p, za, ga, gb = _prompt_inproj(x_prompt, w_in_b, cos_p, sin_p)
    kmean_h = kmean.reshape(b, nb, N_HEADS, HEAD_DIM).transpose(0, 2, 1, 3)
    attT, logits_s = _prompt_attn(qT, kmean_h, kx, vT, page_table, column(q_s), cache_kT)
    y_prompt = _prompt_tail(x_prompt, attT, zb, u_p, za, ga, gb, p_prompt[0], weights)

    psel, ids, den, pnew = _sample_select(logits_s, column(q_s), column(k_s))
    att_s = _sample_pv(page_table, ids[..., 0].reshape(-1), cache_vT, psel, den, pnew, column(v_s))[..., 0]
    state = state_pool[0]
    y_sample = _sample_tail(x_sample[:, 0], att_s.reshape(db, ATT_WIDTH), zb_s, u_s, state.transpose(1, 0, 2),
                            za_s, ga_s, gb_s, p_sample[0][:, 0], weights)

    heads = (N_HEADS, HEAD_DIM)
    pool_prompt = u_p[:, s - POOL_STATE:]
    pool_sample = jnp.concatenate([state[:, 1:], u_s[:, None, :]], axis=1)
    return (y_prompt, y_sample[:, None, :],
            k_p.reshape(1, b, s, *heads), v_p.reshape(1, b, s, *heads), pool_prompt[None],
            k_s.reshape(1, db, 1, *heads), v_s.reshape(1, db, 1, *heads), pool_sample[None])
```

```python
import functools

import numpy as np
import jax
import jax.numpy as jnp
from jax import lax
from jax.experimental import pallas as pl
from jax.experimental.pallas import tpu as pltpu

D_MODEL = 1024
N_HEADS = 8
HEAD_DIM = 64
HALF_DIM = HEAD_DIM // 2
ATT_WIDTH = N_HEADS * HEAD_DIM
POOL_WINDOWS = (2, 4, 8, 16)
POOL_WIDTH = D_MODEL // 2
POOL_GROUP = POOL_WIDTH // len(POOL_WINDOWS)
POOL_STATE = max(POOL_WINDOWS) - 1
MOBA_BLOCK = 256
MOBA_TOPK = 3
PAGE_SIZE = 128
PAGES_PER_BLOCK = MOBA_BLOCK // PAGE_SIZE
ROPE_THETA = 10000.0
PLE_DIM = 256
LN_EPS = 1e-5
DEPTH = 1
DEEPNORM_ALPHA = (2 * DEPTH) ** 0.25
IN_SPLITS = (ATT_WIDTH, ATT_WIDTH, ATT_WIDTH, ATT_WIDTH, POOL_WIDTH, POOL_WIDTH, D_MODEL, D_MODEL)
IN_OFFS = tuple(int(v) for v in np.cumsum((0,) + IN_SPLITS))
IN_WIDTH = IN_OFFS[-1]
Q_SCALE = HEAD_DIM ** -0.5 * float(np.log2(np.e))
SUM_ROWS = 16

LANES = 128
HIST_ROWS = 16
MASK_NEG = -1e30
ROW_TILE = 256
TAIL_TILE = 512
TAIL_PART = 256
VMEM_LIMIT = 52 * 1024 * 1024
ATTN_VMEM_LIMIT = 57 * 1024 * 1024

F32 = jnp.float32
BF16 = jnp.bfloat16


def _sigmoid(x):
    return 0.5 * jnp.tanh(0.5 * x) + 0.5


def _silu(x):
    return x * _sigmoid(x)


def _dot(a, b):
    return jnp.dot(a, b, preferred_element_type=F32)


def _rope(t, cos, sin_signed):
    lane = lax.broadcasted_iota(jnp.int32, (1, LANES), 1)
    first_half = (lane % HEAD_DIM) < HALF_DIM
    outs = []
    for c in range(t.shape[1] // LANES):
        tc = t[:, c * LANES:(c + 1) * LANES]
        partner = jnp.where(first_half, pltpu.roll(tc, LANES - HALF_DIM, 1), pltpu.roll(tc, HALF_DIM, 1))
        outs.append(tc * cos + partner * sin_signed)
    return jnp.concatenate(outs, axis=1)


def _project(xb, w_ref, seg):
    return _dot(xb, w_ref[:, IN_OFFS[seg]:IN_OFFS[seg + 1]])


def _prompt_inproj_kernel(x_ref, w_ref, cos_ref, sin_ref,
                          qT_ref, kx_ref, vT_ref, k_ref, v_ref, km_ref,
                          zb_ref, u_ref, za_ref, ga_ref, gb_ref):
    i = pl.program_id(1)
    xb = x_ref[...].astype(BF16)
    cos = cos_ref[...]
    sin = sin_ref[...]

    q = _rope(_project(xb, w_ref, 0), cos, sin) * Q_SCALE
    qT_ref[...] = q.T.astype(BF16)

    k = _rope(_project(xb, w_ref, 1), cos, sin)
    k_ref[...] = k
    km_ref[...] = jnp.sum(k, axis=0, keepdims=True) / MOBA_BLOCK
    lane = lax.broadcasted_iota(jnp.int32, (ROW_TILE, LANES), 1)
    extra = jnp.where(lane - HEAD_DIM == i, 1.0, 0.0)
    for h in range(N_HEADS):
        pair = k[:, (h // 2) * LANES:(h // 2 + 1) * LANES]
        base = pair if h % 2 == 0 else pltpu.roll(pair, HEAD_DIM, 1)
        kx_ref[h, 0] = jnp.where(lane < HEAD_DIM, base, extra).astype(BF16)

    v = _project(xb, w_ref, 2)
    v_ref[...] = v
    vT = v.T.astype(BF16).reshape(N_HEADS, HEAD_DIM, ROW_TILE)
    vT_ref[:, 0] = jnp.concatenate([vT, jnp.ones((N_HEADS, SUM_ROWS, ROW_TILE), BF16)], axis=1)

    zb_ref[...] = _project(xb, w_ref, 3)
    u_ref[...] = _project(xb, w_ref, 4)
    za_ref[...] = _project(xb, w_ref, 5)
    ga_ref[...] = _project(xb, w_ref, 6)
    gb_ref[...] = _project(xb, w_ref, 7)


def _prompt_inproj(x, w_in, cos, sin):
    b, s, _ = x.shape
    nb = s // MOBA_BLOCK
    assert ROW_TILE == MOBA_BLOCK and s % ROW_TILE == 0 and nb <= 32
    row = lambda width: pl.BlockSpec((None, ROW_TILE, width), lambda bi, i: (bi, i, 0))
    out_shape = [
        jax.ShapeDtypeStruct((b, ATT_WIDTH, s), BF16),
        jax.ShapeDtypeStruct((b, N_HEADS, nb, MOBA_BLOCK, LANES), BF16),
        jax.ShapeDtypeStruct((b, N_HEADS, nb, HEAD_DIM + SUM_ROWS, MOBA_BLOCK), BF16),
        jax.ShapeDtypeStruct((b, s, ATT_WIDTH), F32),
        jax.ShapeDtypeStruct((b, s, ATT_WIDTH), F32),
        jax.ShapeDtypeStruct((b, nb, 1, ATT_WIDTH), F32),
        jax.ShapeDtypeStruct((b, s, ATT_WIDTH), F32),
        jax.ShapeDtypeStruct((b, s, POOL_WIDTH), F32),
        jax.ShapeDtypeStruct((b, s, POOL_WIDTH), F32),
        jax.ShapeDtypeStruct((b, s, D_MODEL), F32),
        jax.ShapeDtypeStruct((b, s, D_MODEL), F32),
    ]
    out_specs = [
        pl.BlockSpec((None, ATT_WIDTH, ROW_TILE), lambda bi, i: (bi, 0, i)),
        pl.BlockSpec((None, N_HEADS, 1, MOBA_BLOCK, LANES), lambda bi, i: (bi, 0, i, 0, 0)),
        pl.BlockSpec((None, N_HEADS, 1, HEAD_DIM + SUM_ROWS, MOBA_BLOCK), lambda bi, i: (bi, 0, i, 0, 0)),
        row(ATT_WIDTH), row(ATT_WIDTH),
        pl.BlockSpec((None, None, 1, ATT_WIDTH), lambda bi, i: (bi, i, 0, 0)),
        row(ATT_WIDTH), row(POOL_WIDTH), row(POOL_WIDTH), row(D_MODEL), row(D_MODEL),
    ]
    in_specs = [
        row(D_MODEL),
        pl.BlockSpec((D_MODEL, IN_WIDTH), lambda bi, i: (0, 0)),
        pl.BlockSpec((ROW_TILE, LANES), lambda bi, i: (i, 0)),
        pl.BlockSpec((ROW_TILE, LANES), lambda bi, i: (i, 0)),
    ]
    return pl.pallas_call(
        _prompt_inproj_kernel,
        grid=(b, s // ROW_TILE),
        in_specs=in_specs, out_specs=out_specs, out_shape=out_shape,
        compiler_params=pltpu.CompilerParams(
            dimension_semantics=("arbitrary", "arbitrary"), vmem_limit_bytes=VMEM_LIMIT),
        name="prompt_inproj",
    )(x, w_in, cos, sin)


def _select_bias(scores, n_past, own):
    blk = lax.broadcasted_iota(jnp.int32, scores.shape, 0)
    n_blk = scores.shape[0]
    s = jnp.where(blk < n_past, scores, -jnp.inf)
    chosen = blk == own
    for _ in range(MOBA_TOPK):
        m = jnp.max(s, axis=0, keepdims=True)
        cand = jnp.where((s == m) & (m > -jnp.inf), blk, n_blk)
        first = jnp.min(cand, axis=0, keepdims=True)
        pick = blk == first
        chosen = chosen | pick
        s = jnp.where(pick, -jnp.inf, s)
    return jnp.where(chosen, 0.0, MASK_NEG)


def _sample_logits_stage(pt_ref, qcol_ref, pool_ref, logit_ref, page_buf, page_sem):
    n_steps = pl.num_programs(0) * pl.num_programs(1)
    g = pl.program_id(0) * pl.num_programs(1) + pl.program_id(1)
    pages_per_step = page_buf.shape[0]
    steps_per_seq = pt_ref.shape[1] // pages_per_step

    def page_copy(step, r):
        seq = step // steps_per_seq
        first = (step % steps_per_seq) * pages_per_step
        return pltpu.make_async_copy(pool_ref.at[0, pt_ref[seq, first + r]], page_buf.at[r], page_sem)

    @pl.when(g == 0)
    def _():
        for r in range(pages_per_step):
            page_copy(0, r).start()

    for r in range(pages_per_step):
        page_copy(g, r).wait()

    rows_per_store = 8
    for h in range(N_HEADS):
        qh = qcol_ref[h]
        for first_blk in range(0, pages_per_step // PAGES_PER_BLOCK, rows_per_store):
            rows = []
            for blk in range(first_blk, first_blk + rows_per_store):
                halves = [jnp.sum(page_buf[blk * PAGES_PER_BLOCK + r, h] * qh, axis=0, keepdims=True)
                          for r in range(PAGES_PER_BLOCK)]
                rows.append(jnp.concatenate(halves, axis=1))
            logit_ref[h, first_blk:first_blk + rows_per_store, :] = jnp.concatenate(rows, axis=0)

    @pl.when(g + 1 < n_steps)
    def _():
        for r in range(pages_per_step):
            page_copy(g + 1, r).start()


def _prompt_attn_kernel(pt_ref, qT_ref, km_ref, kx_ref, vT_ref, qcol_ref, pool_ref, o_ref, logit_ref,
                        qx_scr, s_scr, p_scr, acc_scr, m_scr, alpha_scr, page_buf, page_sem):
    _sample_logits_stage(pt_ref, qcol_ref, pool_ref, logit_ref, page_buf, page_sem)

    i = pl.program_id(1)
    n_blk = km_ref.shape[1]
    heads = range(N_HEADS)

    pad = jnp.zeros((LANES - HEAD_DIM - n_blk, MOBA_BLOCK), BF16)
    for h in heads:
        qT = qT_ref[h * HEAD_DIM:(h + 1) * HEAD_DIM, :]
        bias = _select_bias(_dot(km_ref[h].astype(BF16), qT), i, i)
        qx_scr[h] = jnp.concatenate([qT, bias.astype(BF16), pad], axis=0)
        m_scr[h] = jnp.full((1, MOBA_BLOCK), MASK_NEG, F32)
        acc_scr[h] = jnp.zeros((HEAD_DIM + SUM_ROWS, MOBA_BLOCK), F32)

    def scores(h, blk, st):
        s_scr[st, h] = _dot(kx_ref[h, blk], qx_scr[h])

    def softmax(h, st, keep=None):
        for c in range(MOBA_BLOCK // LANES):
            lanes = slice(c * LANES, (c + 1) * LANES)
            s = s_scr[st, h, :, lanes]
            if keep is not None:
                s = jnp.where(keep[:, lanes], s, MASK_NEG)
            m_old = m_scr[h, :, lanes]
            m_new = jnp.maximum(m_old, jnp.max(s, axis=0, keepdims=True))
            m_scr[h, :, lanes] = m_new
            alpha_scr[h, :, lanes] = jnp.exp2(m_old - m_new)
            p_scr[h, :, lanes] = jnp.exp2(s - m_new).astype(BF16)

    def accumulate(h, blk):
        tiles = [slice(c * LANES, (c + 1) * LANES) for c in range(MOBA_BLOCK // LANES)]
        p = jnp.concatenate([p_scr[h, :, lanes] for lanes in tiles], axis=1)
        alpha = jnp.concatenate([alpha_scr[h, :, lanes] for lanes in tiles], axis=1)
        acc_scr[h] = alpha * acc_scr[h] + _dot(vT_ref[h, blk], p)

    key_row = lax.broadcasted_iota(jnp.int32, (MOBA_BLOCK, MOBA_BLOCK), 0)
    qry_col = lax.broadcasted_iota(jnp.int32, (MOBA_BLOCK, MOBA_BLOCK), 1)
    causal = key_row <= qry_col
    for h in heads:
        scores(h, i, 0)
    for h in heads:
        softmax(h, 0, causal)
    for h in heads:
        scores(h, 0, 1)

    def half_trip(scores_blk, accumulate_blk, st):
        if scores_blk is not None:
            for h in heads:
                scores(h, scores_blk, 1 - st)
        for h in heads:
            accumulate(h, accumulate_blk)
        for h in heads:
            softmax(h, st)

    def pair(k, _):
        t = 2 * k + 1
        half_trip(t, jnp.where(k == 0, i, t - 2), 1)
        half_trip(t + 1, t - 1, 0)
        return 0

    lax.fori_loop(0, i // 2, pair, 0)

    def finish(blk):
        for h in heads:
            accumulate(h, blk)
            acc = acc_scr[h]
            o_ref[h * HEAD_DIM:(h + 1) * HEAD_DIM, :] = acc[:HEAD_DIM] / acc[HEAD_DIM:HEAD_DIM + 1]

    @pl.when(i % 2 == 1)
    def _():
        half_trip(None, jnp.where(i == 1, i, i - 2), 1)
        finish(i - 1)

    @pl.when(i % 2 == 0)
    def _():
        finish(jnp.maximum(i - 1, 0))


def _prompt_attn(qT, kmean, kx, vT, page_table, q_col, cache_kT):
    b, _, s = qT.shape
    nb = s // MOBA_BLOCK
    db, n_pages = page_table.shape
    steps_per_seq, rem = divmod(b * nb, db)
    assert rem == 0 and n_pages % (steps_per_seq * PAGES_PER_BLOCK * 8) == 0
    pages_per_step = n_pages // steps_per_seq
    blocks_per_step = pages_per_step // PAGES_PER_BLOCK
    once = pl.Buffered(1)
    seq_of = lambda bi, i: (bi * nb + i) // steps_per_seq
    grid_spec = pltpu.PrefetchScalarGridSpec(
        num_scalar_prefetch=1,
        grid=(b, nb),
        in_specs=[
            pl.BlockSpec((None, ATT_WIDTH, MOBA_BLOCK), lambda bi, i, pt: (bi, 0, i)),
            pl.BlockSpec((None, N_HEADS, nb, HEAD_DIM), lambda bi, i, pt: (bi, 0, 0, 0)),
            pl.BlockSpec((None, N_HEADS, nb, MOBA_BLOCK, LANES), lambda bi, i, pt: (bi, 0, 0, 0, 0),
                         pipeline_mode=once),
            pl.BlockSpec((None, N_HEADS, nb, HEAD_DIM + SUM_ROWS, MOBA_BLOCK), lambda bi, i, pt: (bi, 0, 0, 0, 0),
                         pipeline_mode=once),
            pl.BlockSpec((None, N_HEADS, HEAD_DIM, PAGE_SIZE), lambda bi, i, pt: (seq_of(bi, i), 0, 0, 0)),
            pl.BlockSpec(memory_space=pl.ANY),
        ],
        out_specs=[
            pl.BlockSpec((None, ATT_WIDTH, MOBA_BLOCK), lambda bi, i, pt: (bi, 0, i)),
            pl.BlockSpec((None, N_HEADS, blocks_per_step, MOBA_BLOCK),
                         lambda bi, i, pt: (seq_of(bi, i), 0, (bi * nb + i) % steps_per_seq, 0)),
        ],
        scratch_shapes=[
            pltpu.VMEM((N_HEADS, LANES, MOBA_BLOCK), BF16),
            pltpu.VMEM((2, N_HEADS, MOBA_BLOCK, MOBA_BLOCK), F32),
            pltpu.VMEM((N_HEADS, MOBA_BLOCK, MOBA_BLOCK), BF16),
            pltpu.VMEM((N_HEADS, HEAD_DIM + SUM_ROWS, MOBA_BLOCK), F32),
            pltpu.VMEM((N_HEADS, 1, MOBA_BLOCK), F32),
            pltpu.VMEM((N_HEADS, 1, MOBA_BLOCK), F32),
            pltpu.VMEM((pages_per_step, N_HEADS, HEAD_DIM, PAGE_SIZE), F32),
            pltpu.SemaphoreType.DMA,
        ],
    )
    return pl.pallas_call(
        _prompt_attn_kernel,
        grid_spec=grid_spec,
        out_shape=[jax.ShapeDtypeStruct((b, ATT_WIDTH, s), F32),
                   jax.ShapeDtypeStruct((db, N_HEADS, n_pages // PAGES_PER_BLOCK, MOBA_BLOCK), F32)],
        compiler_params=pltpu.CompilerParams(
            dimension_semantics=("arbitrary", "arbitrary"), vmem_limit_bytes=ATTN_VMEM_LIMIT),
        name="prompt_attn",
    )(page_table, qT, kmean, kx, vT, q_col, cache_kT)


def _pool_mix(diffs, wmix_ref, scale):
    mixed = [_dot(d.astype(BF16), wmix_ref[g]) for g, d in enumerate(diffs)]
    return jnp.concatenate(mixed, axis=1) * scale


def _layer_norm(r, g_ref, b_ref):
    mu = jnp.mean(r, axis=-1, keepdims=True)
    c = r - mu
    var = jnp.mean(c * c, axis=-1, keepdims=True)
    return c * lax.rsqrt(var + LN_EPS) * g_ref[...] + b_ref[...]


def _layer_tail_parts(parts, wpo_ref, wao_ref, wo_ref, g_ref, b_ref, wple_ref, wgate_ref):
    n = range(len(parts))
    d = [part() for part in parts]
    a = [(d[k]["pool"] * _silu(d[k]["za"])).astype(BF16) for k in n]
    bq = [(d[k]["att"] * _silu(d[k]["zb"])).astype(BF16) for k in n]
    y_pool = [_dot(a[k], wpo_ref[...]) for k in n]
    y_att = [_dot(bq[k], wao_ref[...]) for k in n]
    merged = [(_sigmoid(d[k]["ga"]) * y_pool[k] + _sigmoid(d[k]["gb"]) * y_att[k]).astype(BF16) for k in n]
    r = [DEEPNORM_ALPHA * d[k]["x"] + _dot(merged[k], wo_ref[...]) for k in n]
    h = [_layer_norm(r[k], g_ref, b_ref) for k in n]
    gate = [_sigmoid(_dot(h[k].astype(BF16), wgate_ref[...])) for k in n]
    return [h[k] + gate[k] * _dot(d[k]["p"].astype(BF16), wple_ref[...]) for k in n]


def _prompt_tail_kernel(x_ref, attT_ref, zb_ref, u_ref, uprev_ref, za_ref, ga_ref, gb_ref, p_ref,
                        wmix_ref, scale_ref, wpo_ref, wao_ref, wo_ref, g_ref, b_ref, wple_ref, wgate_ref,
                        y_ref):
    i = pl.program_id(1)
    u = u_ref[...]
    hist = jnp.where(i > 0, uprev_ref[...], 0.0)
    ext = jnp.concatenate([hist, u], axis=0)
    pos = i * TAIL_TILE + lax.broadcasted_iota(jnp.int32, (TAIL_TILE, POOL_GROUP), 0)
    diffs = []
    for g, w in enumerate(POOL_WINDOWS):
        e = ext[:, g * POOL_GROUP:(g + 1) * POOL_GROUP]
        step = 1
        while step < w:
            e = e + pltpu.roll(e, step, 0)
            step *= 2
        cnt = jnp.minimum(pos + 1, w).astype(F32)
        diffs.append(e[HIST_ROWS:] / cnt - u[:, g * POOL_GROUP:(g + 1) * POOL_GROUP])
    row_parts = [slice(r0, r0 + TAIL_PART) for r0 in range(0, TAIL_TILE, TAIL_PART)]

    def loader(rows):
        return lambda: dict(
            x=x_ref[rows, :], att=attT_ref[:, rows].T, zb=zb_ref[rows, :], za=za_ref[rows, :],
            pool=_pool_mix([d[rows] for d in diffs], wmix_ref, scale_ref[...]),
            ga=ga_ref[rows, :], gb=gb_ref[rows, :], p=p_ref[rows, :])

    outs = _layer_tail_parts([loader(rows) for rows in row_parts], wpo_ref, wao_ref, wo_ref, g_ref, b_ref,
                             wple_ref, wgate_ref)
    for rows, out in zip(row_parts, outs):
        y_ref[rows, :] = out


def _full(shape):
    return pl.BlockSpec(shape, lambda *_: (0,) * len(shape))


def _tail_weight_specs():
    return [
        _full((len(POOL_WINDOWS), POOL_GROUP, POOL_GROUP)), _full((1, POOL_WIDTH)),
        _full((POOL_WIDTH, D_MODEL)), _full((ATT_WIDTH, D_MODEL)), _full((D_MODEL, D_MODEL)),
        _full((1, D_MODEL)), _full((1, D_MODEL)), _full((PLE_DIM, D_MODEL)), _full((D_MODEL, D_MODEL)),
    ]


def _prompt_tail(x, attT, zb, u, za, ga, gb, p, weights):
    b, s, _ = x.shape
    assert s % TAIL_TILE == 0
    row = lambda width: pl.BlockSpec((None, TAIL_TILE, width), lambda bi, i: (bi, i, 0))
    hist_blocks = TAIL_TILE // HIST_ROWS
    in_specs = [
        row(D_MODEL),
        pl.BlockSpec((None, ATT_WIDTH, TAIL_TILE), lambda bi, i: (bi, 0, i)),
        row(ATT_WIDTH), row(POOL_WIDTH),
        pl.BlockSpec((None, HIST_ROWS, POOL_WIDTH), lambda bi, i: (bi, jnp.maximum(i * hist_blocks - 1, 0), 0)),
        row(POOL_WIDTH), row(D_MODEL), row(D_MODEL), row(PLE_DIM),
    ] + _tail_weight_specs()
    return pl.pallas_call(
        _prompt_tail_kernel,
        grid=(b, s // TAIL_TILE),
        in_specs=in_specs,
        out_specs=row(D_MODEL),
        out_shape=jax.ShapeDtypeStruct((b, s, D_MODEL), F32),
        compiler_params=pltpu.CompilerParams(
            dimension_semantics=("arbitrary", "arbitrary"), vmem_limit_bytes=VMEM_LIMIT),
        name="prompt_tail",
    )(x, attT, zb, u, u, za, ga, gb, p, *weights)


def _sample_inproj_kernel(x_ref, w_ref, cos_ref, sin_ref,
                          q_ref, k_ref, v_ref, zb_ref, u_ref, za_ref, ga_ref, gb_ref):
    xb = x_ref[...].astype(BF16)
    cos = cos_ref[...]
    sin = sin_ref[...]
    q_ref[...] = _rope(_project(xb, w_ref, 0), cos, sin) * Q_SCALE
    k_ref[...] = _rope(_project(xb, w_ref, 1), cos, sin)
    v_ref[...] = _project(xb, w_ref, 2)
    zb_ref[...] = _project(xb, w_ref, 3)
    u_ref[...] = _project(xb, w_ref, 4)
    za_ref[...] = _project(xb, w_ref, 5)
    ga_ref[...] = _project(xb, w_ref, 6)
    gb_ref[...] = _project(xb, w_ref, 7)


def _sample_inproj(x, w_in, cos, sin):
    n = x.shape[0]
    widths = (ATT_WIDTH, ATT_WIDTH, ATT_WIDTH, ATT_WIDTH, POOL_WIDTH, POOL_WIDTH, D_MODEL, D_MODEL)
    return pl.pallas_call(
        _sample_inproj_kernel,
        out_shape=[jax.ShapeDtypeStruct((n, w), F32) for w in widths],
        compiler_params=pltpu.CompilerParams(vmem_limit_bytes=VMEM_LIMIT),
        name="sample_inproj",
    )(x, w_in, cos, sin)


def _sample_select_kernel(logit_ref, q_ref, kn_ref, psel_ref, ids_ref, den_ref, pnew_ref):
    logits = logit_ref[...]
    n_blk = logits.shape[1]
    sc = jnp.sum(logits, axis=2, keepdims=True) / MOBA_BLOCK
    blk = lax.broadcasted_iota(jnp.int32, sc.shape, 1)
    picks = []
    for t in range(MOBA_TOPK):
        m = jnp.max(sc, axis=1, keepdims=True)
        first = jnp.min(jnp.where(sc == m, blk, n_blk), axis=1, keepdims=True)
        ids_ref[:, t:t + 1, :] = jnp.broadcast_to(first, (N_HEADS, 1, LANES))
        picks.append(blk == first)
        sc = jnp.where(picks[-1], -jnp.inf, sc)
    chosen = picks[0]
    for pick in picks[1:]:
        chosen = chosen | pick
    s_new = jnp.sum(q_ref[...] * kn_ref[...], axis=1, keepdims=True)[:, :, 0:1]
    masked = jnp.where(chosen, logits, MASK_NEG)
    m = jnp.max(jnp.max(masked, axis=2, keepdims=True), axis=1, keepdims=True)
    m = jnp.maximum(m, s_new)
    p = jnp.exp2(masked - m)
    p_new = jnp.exp2(s_new - m)
    den = jnp.sum(jnp.sum(p, axis=2, keepdims=True), axis=1, keepdims=True) + p_new
    den_ref[...] = jnp.broadcast_to(den, den_ref.shape)
    pnew_ref[...] = jnp.broadcast_to(p_new, pnew_ref.shape)
    for t, pick in enumerate(picks):
        psel_ref[:, t:t + 1, :] = jnp.sum(jnp.where(pick, p, 0.0), axis=1, keepdims=True)


def _sample_select(logits, q, k_new):
    db, _, n_blk, _ = logits.shape
    per_seq = lambda *tail: pl.BlockSpec((None, N_HEADS) + tail, lambda b: (b,) + (0,) * (len(tail) + 1))
    return pl.pallas_call(
        _sample_select_kernel,
        grid=(db,),
        in_specs=[per_seq(n_blk, MOBA_BLOCK), per_seq(HEAD_DIM, PAGE_SIZE), per_seq(HEAD_DIM, PAGE_SIZE)],
        out_specs=[per_seq(MOBA_TOPK, MOBA_BLOCK), per_seq(MOBA_TOPK, LANES), per_seq(1, LANES), per_seq(1, LANES)],
        out_shape=[jax.ShapeDtypeStruct((db, N_HEADS, MOBA_TOPK, MOBA_BLOCK), F32),
                   jax.ShapeDtypeStruct((db, N_HEADS, MOBA_TOPK, LANES), jnp.int32),
                   jax.ShapeDtypeStruct((db, N_HEADS, 1, LANES), F32),
                   jax.ShapeDtypeStruct((db, N_HEADS, 1, LANES), F32)],
        compiler_params=pltpu.CompilerParams(dimension_semantics=("arbitrary",), vmem_limit_bytes=VMEM_LIMIT),
        name="sample_select",
    )(logits, q, k_new)


PV_HEADS = 8


def _sample_pv_kernel(pt_ref, ids_ref, *refs):
    per_head = MOBA_TOPK * PAGES_PER_BLOCK
    slabs = refs[:PV_HEADS * per_head]
    psel_ref, den_ref, pnew_ref, vn_ref, o_ref = refs[PV_HEADS * per_head:]
    for hh in range(PV_HEADS):
        acc = pnew_ref[hh] * vn_ref[hh] * (1.0 / PAGE_SIZE)
        for t in range(MOBA_TOPK):
            for r in range(PAGES_PER_BLOCK):
                slab = slabs[(hh * MOBA_TOPK + t) * PAGES_PER_BLOCK + r][...]
                acc = acc + slab * psel_ref[hh, t:t + 1, r * PAGE_SIZE:(r + 1) * PAGE_SIZE]
        out = jnp.sum(acc, axis=1, keepdims=True) / den_ref[hh, :, 0:1]
        o_ref[hh] = jnp.broadcast_to(out, o_ref.shape[1:])


def _sample_pv(page_table, ids, cache_vT, psel, den, pnew, v_new):
    db = page_table.shape[0]

    def slab_spec(hh, t, r):
        def index(b, hg, pt, ids):
            h = hg * PV_HEADS + hh
            blk = jnp.clip(ids[(b * N_HEADS + h) * MOBA_TOPK + t], 0, pt.shape[1] // PAGES_PER_BLOCK - 1)
            return (0, pt[b, blk * PAGES_PER_BLOCK + r], h, 0, 0)
        return pl.BlockSpec((None, None, None, HEAD_DIM, PAGE_SIZE), index)

    slab_specs = [slab_spec(hh, t, r) for hh in range(PV_HEADS) for t in range(MOBA_TOPK)
                  for r in range(PAGES_PER_BLOCK)]
    group = lambda *tail: pl.BlockSpec((None, PV_HEADS) + tail, lambda b, hg, pt, ids: (b, hg) + (0,) * len(tail))
    grid_spec = pltpu.PrefetchScalarGridSpec(
        num_scalar_prefetch=2,
        grid=(db, N_HEADS // PV_HEADS),
        in_specs=slab_specs + [group(MOBA_TOPK, MOBA_BLOCK), group(1, LANES), group(1, LANES),
                               group(HEAD_DIM, PAGE_SIZE)],
        out_specs=group(HEAD_DIM, LANES),
    )
    return pl.pallas_call(
        _sample_pv_kernel,
        grid_spec=grid_spec,
        out_shape=jax.ShapeDtypeStruct((db, N_HEADS, HEAD_DIM, LANES), F32),
        compiler_params=pltpu.CompilerParams(
            dimension_semantics=("arbitrary", "arbitrary"), vmem_limit_bytes=VMEM_LIMIT),
        name="sample_pv",
    )(page_table, ids, *([cache_vT] * len(slab_specs)), psel, den, pnew, v_new)


def _sample_tail_kernel(x_ref, att_ref, zb_ref, u_ref, state_ref, za_ref, ga_ref, gb_ref, p_ref,
                        wmix_ref, scale_ref, wpo_ref, wao_ref, wo_ref, g_ref, b_ref, wple_ref, wgate_ref,
                        y_ref):
    u = u_ref[...]
    diffs = []
    for g, w in enumerate(POOL_WINDOWS):
        lanes = slice(g * POOL_GROUP, (g + 1) * POOL_GROUP)
        total = u[:, lanes]
        for back in range(1, w):
            total = total + state_ref[POOL_STATE - back, :, lanes]
        diffs.append(total / float(w) - u[:, lanes])
    whole = lambda: dict(x=x_ref[...], att=att_ref[...], zb=zb_ref[...], za=za_ref[...],
                         pool=_pool_mix(diffs, wmix_ref, scale_ref[...]),
                         ga=ga_ref[...], gb=gb_ref[...], p=p_ref[...])
    y_ref[...] = _layer_tail_parts([whole], wpo_ref, wao_ref, wo_ref, g_ref, b_ref, wple_ref, wgate_ref)[0]


def _sample_tail(x, att, zb, u, state_t, za, ga, gb, p, weights):
    return pl.pallas_call(
        _sample_tail_kernel,
        out_shape=jax.ShapeDtypeStruct(x.shape, F32),
        compiler_params=pltpu.CompilerParams(vmem_limit_bytes=VMEM_LIMIT),
        name="sample_tail",
    )(x, att, zb, u, state_t, za, ga, gb, p, *weights)


def _rope_tables(pos):
    inv = ROPE_THETA ** (-jnp.arange(HALF_DIM, dtype=F32) / HALF_DIM)
    ang = pos.astype(F32)[:, None] * inv[None, :]
    cos = jnp.cos(ang)
    sin = jnp.sin(ang)
    return jnp.tile(cos, (1, 4)), jnp.tile(jnp.concatenate([-sin, sin], axis=1), (1, 2))


def kernel(x_prompt, x_sample, cache_k, cache_v, state_pool, page_table, p_prompt, p_sample, w_in, w_pool_mix, pool_scale, w_pool_out, w_att_out, w_o, ln_g, ln_b, w_ple, w_ple_gate):
    assert w_in.shape[0] == DEPTH
    b, s, _ = x_prompt.shape
    db, n, _ = x_sample.shape
    assert n == 1
    n_pages = page_table.shape[1]
    past_len = n_pages * PAGE_SIZE
    assert past_len % MOBA_BLOCK == 0 and past_len // MOBA_BLOCK >= MOBA_TOPK
    nb = s // MOBA_BLOCK

    w_in_b = w_in[0].astype(BF16)
    weights = (w_pool_mix[0].astype(BF16), pool_scale[0][None, :], w_pool_out[0].astype(BF16),
               w_att_out[0].astype(BF16), w_o[0].astype(BF16), ln_g[0][None, :], ln_b[0][None, :],
               w_ple[0].astype(BF16), w_ple_gate[0].astype(BF16))

    cos_s, sin_s = _rope_tables(jnp.full((1,), past_len))
    q_s, k_s, v_s, zb_s, u_s, za_s, ga_s, gb_s = _sample_inproj(x_sample[:, 0], w_in_b, cos_s, sin_s)
    cache_kT = cache_k.transpose(0, 1, 3, 4, 2)
    cache_vT = cache_v.transpose(0, 1, 3, 4, 2)
    column = lambda t: jnp.broadcast_to(t.reshape(db, N_HEADS, HEAD_DIM, 1), (db, N_HEADS, HEAD_DIM, PAGE_SIZE))

    cos_p, sin_p = _rope_tables(jnp.arange(s))
    qT, kx, vT, k_p, v_p, kmean, zb, u_p, za, ga, gb = _prompt_inproj(x_prompt, w_in_b, cos_p, sin_p)
    kmean_h = kmean.reshape(b, nb, N_HEADS, HEAD_DIM).transpose(0, 2, 1, 3)
    attT, logits_s = _prompt_attn(qT, kmean_h, kx, vT, page_table, column(q_s), cache_kT)
    y_prompt = _prompt_tail(x_prompt, attT, zb, u_p, za, ga, gb, p_prompt[0], weights)

    psel, ids, den, pnew = _sample_select(logits_s, column(q_s), column(k_s))
    att_s = _sample_pv(page_table, ids[..., 0].reshape(-1), cache_vT, psel, den, pnew, column(v_s))[..., 0]
    state = state_pool[0]
    y_sample = _sample_tail(x_sample[:, 0], att_s.reshape(db, ATT_WIDTH), zb_s, u_s, state.transpose(1, 0, 2),
                            za_s, ga_s, gb_s, p_sample[0][:, 0], weights)

    heads = (N_HEADS, HEAD_DIM)
    pool_prompt = u_p[:, s - POOL_STATE:]
    pool_sample = jnp.concatenate([state[:, 1:], u_s[:, None, :]], axis=1)
    return (y_prompt, y_sample[:, None, :],
            k_p.reshape(1, b, s, *heads), v_p.reshape(1, b, s, *heads), pool_prompt[None],
            k_s.reshape(1, db, 1, *heads), v_s.reshape(1, db, 1, *heads), pool_sample[None])
```

```python
import functools

import numpy as np
import jax
import jax.numpy as jnp
from jax import lax
from jax.experimental import pallas as pl
from jax.experimental.pallas import tpu as pltpu

D_MODEL = 1024
N_HEADS = 8
HEAD_DIM = 64
HALF_DIM = HEAD_DIM // 2
ATT_WIDTH = N_HEADS * HEAD_DIM
POOL_WINDOWS = (2, 4, 8, 16)
POOL_WIDTH = D_MODEL // 2
POOL_GROUP = POOL_WIDTH // len(POOL_WINDOWS)
POOL_STATE = max(POOL_WINDOWS) - 1
MOBA_BLOCK = 256
MOBA_TOPK = 3
PAGE_SIZE = 128
PAGES_PER_BLOCK = MOBA_BLOCK // PAGE_SIZE
ROPE_THETA = 10000.0
PLE_DIM = 256
LN_EPS = 1e-5
DEPTH = 1
DEEPNORM_ALPHA = (2 * DEPTH) ** 0.25
IN_SPLITS = (ATT_WIDTH, ATT_WIDTH, ATT_WIDTH, ATT_WIDTH, POOL_WIDTH, POOL_WIDTH, D_MODEL, D_MODEL)
IN_OFFS = tuple(int(v) for v in np.cumsum((0,) + IN_SPLITS))
IN_WIDTH = IN_OFFS[-1]
Q_SCALE = HEAD_DIM ** -0.5 * float(np.log2(np.e))
SUM_ROWS = 16

LANES = 128
HIST_ROWS = 16
MASK_NEG = -1e30
ROW_TILE = 256
TAIL_TILE = 512
TAIL_PART = 256
VMEM_LIMIT = 52 * 1024 * 1024
ATTN_VMEM_LIMIT = 57 * 1024 * 1024

F32 = jnp.float32
BF16 = jnp.bfloat16


def _sigmoid(x):
    return 0.5 * jnp.tanh(0.5 * x) + 0.5


def _silu(x):
    return x * _sigmoid(x)


def _dot(a, b):
    return jnp.dot(a, b, preferred_element_type=F32)


def _rope(t, cos, sin_signed):
    lane = lax.broadcasted_iota(jnp.int32, (1, LANES), 1)
    first_half = (lane % HEAD_DIM) < HALF_DIM
    outs = []
    for c in range(t.shape[1] // LANES):
        tc = t[:, c * LANES:(c + 1) * LANES]
        partner = jnp.where(first_half, pltpu.roll(tc, LANES - HALF_DIM, 1), pltpu.roll(tc, HALF_DIM, 1))
        outs.append(tc * cos + partner * sin_signed)
    return jnp.concatenate(outs, axis=1)


def _project(xb, w_ref, seg):
    return _dot(xb, w_ref[:, IN_OFFS[seg]:IN_OFFS[seg + 1]])


def _prompt_inproj_kernel(x_ref, w_ref, cos_ref, sin_ref,
                          qT_ref, kx_ref, vT_ref, k_ref, v_ref, km_ref,
                          zb_ref, u_ref, za_ref, ga_ref, gb_ref):
    i = pl.program_id(1)
    xb = x_ref[...].astype(BF16)
    cos = cos_ref[...]
    sin = sin_ref[...]

    q = _rope(_project(xb, w_ref, 0), cos, sin) * Q_SCALE
    qT_ref[...] = q.T.astype(BF16)

    k = _rope(_project(xb, w_ref, 1), cos, sin)
    k_ref[...] = k
    km_ref[...] = jnp.sum(k, axis=0, keepdims=True) / MOBA_BLOCK
    lane = lax.broadcasted_iota(jnp.int32, (ROW_TILE, LANES), 1)
    extra = jnp.where(lane - HEAD_DIM == i, 1.0, 0.0)
    for h in range(N_HEADS):
        pair = k[:, (h // 2) * LANES:(h // 2 + 1) * LANES]
        base = pair if h % 2 == 0 else pltpu.roll(pair, HEAD_DIM, 1)
        kx_ref[h, 0] = jnp.where(lane < HEAD_DIM, base, extra).astype(BF16)

    v = _project(xb, w_ref, 2)
    v_ref[...] = v
    vT = v.T.astype(BF16).reshape(N_HEADS, HEAD_DIM, ROW_TILE)
    vT_ref[:, 0] = jnp.concatenate([vT, jnp.ones((N_HEADS, SUM_ROWS, ROW_TILE), BF16)], axis=1)

    zb_ref[...] = _project(xb, w_ref, 3)
    u_ref[...] = _project(xb, w_ref, 4)
    za_ref[...] = _project(xb, w_ref, 5)
    ga_ref[...] = _project(xb, w_ref, 6)
    gb_ref[...] = _project(xb, w_ref, 7)


def _prompt_inproj(x, w_in, cos, sin):
    b, s, _ = x.shape
    nb = s // MOBA_BLOCK
    assert ROW_TILE == MOBA_BLOCK and s % ROW_TILE == 0 and nb <= 32
    row = lambda width: pl.BlockSpec((None, ROW_TILE, width), lambda bi, i: (bi, i, 0))
    out_shape = [
        jax.ShapeDtypeStruct((b, ATT_WIDTH, s), BF16),
        jax.ShapeDtypeStruct((b, N_HEADS, nb, MOBA_BLOCK, LANES), BF16),
        jax.ShapeDtypeStruct((b, N_HEADS, nb, HEAD_DIM + SUM_ROWS, MOBA_BLOCK), BF16),
        jax.ShapeDtypeStruct((b, s, ATT_WIDTH), F32),
        jax.ShapeDtypeStruct((b, s, ATT_WIDTH), F32),
        jax.ShapeDtypeStruct((b, nb, 1, ATT_WIDTH), F32),
        jax.ShapeDtypeStruct((b, s, ATT_WIDTH), F32),
        jax.ShapeDtypeStruct((b, s, POOL_WIDTH), F32),
        jax.ShapeDtypeStruct((b, s, POOL_WIDTH), F32),
        jax.ShapeDtypeStruct((b, s, D_MODEL), F32),
        jax.ShapeDtypeStruct((b, s, D_MODEL), F32),
    ]
    out_specs = [
        pl.BlockSpec((None, ATT_WIDTH, ROW_TILE), lambda bi, i: (bi, 0, i)),
        pl.BlockSpec((None, N_HEADS, 1, MOBA_BLOCK, LANES), lambda bi, i: (bi, 0, i, 0, 0)),
        pl.BlockSpec((None, N_HEADS, 1, HEAD_DIM + SUM_ROWS, MOBA_BLOCK), lambda bi, i: (bi, 0, i, 0, 0)),
        row(ATT_WIDTH), row(ATT_WIDTH),
        pl.BlockSpec((None, None, 1, ATT_WIDTH), lambda bi, i: (bi, i, 0, 0)),
        row(ATT_WIDTH), row(POOL_WIDTH), row(POOL_WIDTH), row(D_MODEL), row(D_MODEL),
    ]
    in_specs = [
        row(D_MODEL),
        pl.BlockSpec((D_MODEL, IN_WIDTH), lambda bi, i: (0, 0)),
        pl.BlockSpec((ROW_TILE, LANES), lambda bi, i: (i, 0)),
        pl.BlockSpec((ROW_TILE, LANES), lambda bi, i: (i, 0)),
    ]
    return pl.pallas_call(
        _prompt_inproj_kernel,
        grid=(b, s // ROW_TILE),
        in_specs=in_specs, out_specs=out_specs, out_shape=out_shape,
        compiler_params=pltpu.CompilerParams(
            dimension_semantics=("arbitrary", "arbitrary"), vmem_limit_bytes=VMEM_LIMIT),
        name="prompt_inproj",
    )(x, w_in, cos, sin)


def _select_bias(scores, n_past, own):
    blk = lax.broadcasted_iota(jnp.int32, scores.shape, 0)
    n_blk = scores.shape[0]
    s = jnp.where(blk < n_past, scores, -jnp.inf)
    chosen = blk == own
    for _ in range(MOBA_TOPK):
        m = jnp.max(s, axis=0, keepdims=True)
        cand = jnp.where((s == m) & (m > -jnp.inf), blk, n_blk)
        first = jnp.min(cand, axis=0, keepdims=True)
        pick = blk == first
        chosen = chosen | pick
        s = jnp.where(pick, -jnp.inf, s)
    return jnp.where(chosen, 0.0, MASK_NEG)


def _sample_logits_stage(pt_ref, qcol_ref, pool_ref, logit_ref, page_buf, page_sem):
    n_steps = pl.num_programs(0) * pl.num_programs(1)
    g = pl.program_id(0) * pl.num_programs(1) + pl.program_id(1)
    pages_per_step = page_buf.shape[0]
    steps_per_seq = pt_ref.shape[1] // pages_per_step

    def page_copy(step, r):
        seq = step // steps_per_seq
        first = (step % steps_per_seq) * pages_per_step
        return pltpu.make_async_copy(pool_ref.at[0, pt_ref[seq, first + r]], page_buf.at[r], page_sem)

    @pl.when(g == 0)
    def _():
        for r in range(pages_per_step):
            page_copy(0, r).start()

    for r in range(pages_per_step):
        page_copy(g, r).wait()

    rows_per_store = 8
    blk_base = (g % steps_per_seq) * (pages_per_step // PAGES_PER_BLOCK)
    for h in range(N_HEADS):
        qh = qcol_ref[h]
        for first_blk in range(0, pages_per_step // PAGES_PER_BLOCK, rows_per_store):
            rows = []
            for blk in range(first_blk, first_blk + rows_per_store):
                halves = [jnp.sum(page_buf[blk * PAGES_PER_BLOCK + r, h] * qh, axis=0, keepdims=True)
                          for r in range(PAGES_PER_BLOCK)]
                rows.append(jnp.concatenate(halves, axis=1))
            row0 = pl.multiple_of(blk_base + first_blk, rows_per_store)
            logit_ref[h, pl.ds(row0, rows_per_store), :] = jnp.concatenate(rows, axis=0)

    @pl.when(g + 1 < n_steps)
    def _():
        for r in range(pages_per_step):
            page_copy(g + 1, r).start()


def _prompt_attn_kernel(pt_ref, qT_ref, km_ref, kx_ref, vT_ref, qcol_ref, kncol_ref, pool_ref,
                        o_ref, psel_ref, ids_ref, den_ref, pnew_ref,
                        qx_scr, s_scr, p_scr, acc_scr, m_scr, alpha_scr, page_buf, page_sem, logit_ref):
    _sample_logits_stage(pt_ref, qcol_ref, pool_ref, logit_ref, page_buf, page_sem)
    steps_per_seq = pt_ref.shape[1] // page_buf.shape[0]
    g = pl.program_id(0) * pl.num_programs(1) + pl.program_id(1)

    @pl.when(g % steps_per_seq == steps_per_seq - 1)
    def _():
        _sample_select_kernel(logit_ref, qcol_ref, kncol_ref, psel_ref, ids_ref, den_ref, pnew_ref)

    i = pl.program_id(1)
    n_blk = km_ref.shape[1]
    heads = range(N_HEADS)

    pad = jnp.zeros((LANES - HEAD_DIM - n_blk, MOBA_BLOCK), BF16)
    for h in heads:
        qT = qT_ref[h * HEAD_DIM:(h + 1) * HEAD_DIM, :]
        bias = _select_bias(_dot(km_ref[h].astype(BF16), qT), i, i)
        qx_scr[h] = jnp.concatenate([qT, bias.astype(BF16), pad], axis=0)
        m_scr[h] = jnp.full((1, MOBA_BLOCK), MASK_NEG, F32)
        acc_scr[h] = jnp.zeros((HEAD_DIM + SUM_ROWS, MOBA_BLOCK), F32)

    def scores(h, blk, st):
        s_scr[st, h] = _dot(kx_ref[h, blk], qx_scr[h])

    def softmax(h, st, keep=None):
        for c in range(MOBA_BLOCK // LANES):
            lanes = slice(c * LANES, (c + 1) * LANES)
            s = s_scr[st, h, :, lanes]
            if keep is not None:
                s = jnp.where(keep[:, lanes], s, MASK_NEG)
            m_old = m_scr[h, :, lanes]
            m_new = jnp.maximum(m_old, jnp.max(s, axis=0, keepdims=True))
            m_scr[h, :, lanes] = m_new
            alpha_scr[h, :, lanes] = jnp.exp2(m_old - m_new)
            p_scr[h, :, lanes] = jnp.exp2(s - m_new).astype(BF16)

    def accumulate(h, blk):
        tiles = [slice(c * LANES, (c + 1) * LANES) for c in range(MOBA_BLOCK // LANES)]
        p = jnp.concatenate([p_scr[h, :, lanes] for lanes in tiles], axis=1)
        alpha = jnp.concatenate([alpha_scr[h, :, lanes] for lanes in tiles], axis=1)
        acc_scr[h] = alpha * acc_scr[h] + _dot(vT_ref[h, blk], p)

    key_row = lax.broadcasted_iota(jnp.int32, (MOBA_BLOCK, MOBA_BLOCK), 0)
    qry_col = lax.broadcasted_iota(jnp.int32, (MOBA_BLOCK, MOBA_BLOCK), 1)
    causal = key_row <= qry_col
    for h in heads:
        scores(h, i, 0)
    for h in heads:
        softmax(h, 0, causal)
    for h in heads:
        scores(h, 0, 1)

    def half_trip(scores_blk, accumulate_blk, st):
        if scores_blk is not None:
            for h in heads:
                scores(h, scores_blk, 1 - st)
        for h in heads:
            accumulate(h, accumulate_blk)
        for h in heads:
            softmax(h, st)

    def pair(k, _):
        t = 2 * k + 1
        half_trip(t, jnp.where(k == 0, i, t - 2), 1)
        half_trip(t + 1, t - 1, 0)
        return 0

    lax.fori_loop(0, i // 2, pair, 0)

    def finish(blk):
        for h in heads:
            accumulate(h, blk)
            acc = acc_scr[h]
            o_ref[h * HEAD_DIM:(h + 1) * HEAD_DIM, :] = acc[:HEAD_DIM] / acc[HEAD_DIM:HEAD_DIM + 1]

    @pl.when(i % 2 == 1)
    def _():
        half_trip(None, jnp.where(i == 1, i, i - 2), 1)
        finish(i - 1)

    @pl.when(i % 2 == 0)
    def _():
        finish(jnp.maximum(i - 1, 0))


def _prompt_attn(qT, kmean, kx, vT, page_table, q_col, k_col, cache_kT):
    b, _, s = qT.shape
    nb = s // MOBA_BLOCK
    db, n_pages = page_table.shape
    steps_per_seq, rem = divmod(b * nb, db)
    assert rem == 0 and n_pages % (steps_per_seq * PAGES_PER_BLOCK * 8) == 0
    pages_per_step = n_pages // steps_per_seq
    blocks_per_step = pages_per_step // PAGES_PER_BLOCK
    once = pl.Buffered(1)
    seq_of = lambda bi, i: (bi * nb + i) // steps_per_seq
    per_seq = lambda *tail: pl.BlockSpec((None, N_HEADS) + tail,
                                         lambda bi, i, pt: (seq_of(bi, i),) + (0,) * (len(tail) + 1))
    grid_spec = pltpu.PrefetchScalarGridSpec(
        num_scalar_prefetch=1,
        grid=(b, nb),
        in_specs=[
            pl.BlockSpec((None, ATT_WIDTH, MOBA_BLOCK), lambda bi, i, pt: (bi, 0, i)),
            pl.BlockSpec((None, N_HEADS, nb, HEAD_DIM), lambda bi, i, pt: (bi, 0, 0, 0)),
            pl.BlockSpec((None, N_HEADS, nb, MOBA_BLOCK, LANES), lambda bi, i, pt: (bi, 0, 0, 0, 0),
                         pipeline_mode=once),
            pl.BlockSpec((None, N_HEADS, nb, HEAD_DIM + SUM_ROWS, MOBA_BLOCK), lambda bi, i, pt: (bi, 0, 0, 0, 0),
                         pipeline_mode=once),
            per_seq(HEAD_DIM, PAGE_SIZE), per_seq(HEAD_DIM, PAGE_SIZE),
            pl.BlockSpec(memory_space=pl.ANY),
        ],
        out_specs=[
            pl.BlockSpec((None, ATT_WIDTH, MOBA_BLOCK), lambda bi, i, pt: (bi, 0, i)),
            per_seq(MOBA_TOPK, MOBA_BLOCK), per_seq(MOBA_TOPK, LANES), per_seq(1, LANES), per_seq(1, LANES),
        ],
        scratch_shapes=[
            pltpu.VMEM((N_HEADS, LANES, MOBA_BLOCK), BF16),
            pltpu.VMEM((2, N_HEADS, MOBA_BLOCK, MOBA_BLOCK), F32),
            pltpu.VMEM((N_HEADS, MOBA_BLOCK, MOBA_BLOCK), BF16),
            pltpu.VMEM((N_HEADS, HEAD_DIM + SUM_ROWS, MOBA_BLOCK), F32),
            pltpu.VMEM((N_HEADS, 1, MOBA_BLOCK), F32),
            pltpu.VMEM((N_HEADS, 1, MOBA_BLOCK), F32),
            pltpu.VMEM((pages_per_step, N_HEADS, HEAD_DIM, PAGE_SIZE), F32),
            pltpu.SemaphoreType.DMA,
            pltpu.VMEM((N_HEADS, n_pages // PAGES_PER_BLOCK, MOBA_BLOCK), F32),
        ],
    )
    return pl.pallas_call(
        _prompt_attn_kernel,
        grid_spec=grid_spec,
        out_shape=[jax.ShapeDtypeStruct((b, ATT_WIDTH, s), F32),
                   jax.ShapeDtypeStruct((db, N_HEADS, MOBA_TOPK, MOBA_BLOCK), F32),
                   jax.ShapeDtypeStruct((db, N_HEADS, MOBA_TOPK, LANES), jnp.int32),
                   jax.ShapeDtypeStruct((db, N_HEADS, 1, LANES), F32),
                   jax.ShapeDtypeStruct((db, N_HEADS, 1, LANES), F32)],
        compiler_params=pltpu.CompilerParams(
            dimension_semantics=("arbitrary", "arbitrary"), vmem_limit_bytes=ATTN_VMEM_LIMIT),
        name="prompt_attn",
    )(page_table, qT, kmean, kx, vT, q_col, k_col, cache_kT)


def _pool_mix(diffs, wmix_ref, scale):
    mixed = [_dot(d.astype(BF16), wmix_ref[g]) for g, d in enumerate(diffs)]
    return jnp.concatenate(mixed, axis=1) * scale


def _layer_norm(r, g_ref, b_ref):
    mu = jnp.mean(r, axis=-1, keepdims=True)
    c = r - mu
    var = jnp.mean(c * c, axis=-1, keepdims=True)
    return c * lax.rsqrt(var + LN_EPS) * g_ref[...] + b_ref[...]


def _layer_tail_parts(parts, wpo_ref, wao_ref, wo_ref, g_ref, b_ref, wple_ref, wgate_ref):
    n = range(len(parts))
    d = [part() for part in parts]
    a = [(d[k]["pool"] * _silu(d[k]["za"])).astype(BF16) for k in n]
    bq = [(d[k]["att"] * _silu(d[k]["zb"])).astype(BF16) for k in n]
    y_pool = [_dot(a[k], wpo_ref[...]) for k in n]
    y_att = [_dot(bq[k], wao_ref[...]) for k in n]
    merged = [(_sigmoid(d[k]["ga"]) * y_pool[k] + _sigmoid(d[k]["gb"]) * y_att[k]).astype(BF16) for k in n]
    r = [DEEPNORM_ALPHA * d[k]["x"] + _dot(merged[k], wo_ref[...]) for k in n]
    h = [_layer_norm(r[k], g_ref, b_ref) for k in n]
    gate = [_sigmoid(_dot(h[k].astype(BF16), wgate_ref[...])) for k in n]
    return [h[k] + gate[k] * _dot(d[k]["p"].astype(BF16), wple_ref[...]) for k in n]


def _prompt_tail_kernel(x_ref, attT_ref, zb_ref, u_ref, uprev_ref, za_ref, ga_ref, gb_ref, p_ref,
                        wmix_ref, scale_ref, wpo_ref, wao_ref, wo_ref, g_ref, b_ref, wple_ref, wgate_ref,
                        y_ref):
    i = pl.program_id(1)
    u = u_ref[...]
    hist = jnp.where(i > 0, uprev_ref[...], 0.0)
    ext = jnp.concatenate([hist, u], axis=0)
    pos = i * TAIL_TILE + lax.broadcasted_iota(jnp.int32, (TAIL_TILE, POOL_GROUP), 0)
    diffs = []
    for g, w in enumerate(POOL_WINDOWS):
        e = ext[:, g * POOL_GROUP:(g + 1) * POOL_GROUP]
        step = 1
        while step < w:
            e = e + pltpu.roll(e, step, 0)
            step *= 2
        cnt = jnp.minimum(pos + 1, w).astype(F32)
        diffs.append(e[HIST_ROWS:] / cnt - u[:, g * POOL_GROUP:(g + 1) * POOL_GROUP])
    row_parts = [slice(r0, r0 + TAIL_PART) for r0 in range(0, TAIL_TILE, TAIL_PART)]

    def loader(rows):
        return lambda: dict(
            x=x_ref[rows, :], att=attT_ref[:, rows].T, zb=zb_ref[rows, :], za=za_ref[rows, :],
            pool=_pool_mix([d[rows] for d in diffs], wmix_ref, scale_ref[...]),
            ga=ga_ref[rows, :], gb=gb_ref[rows, :], p=p_ref[rows, :])

    outs = _layer_tail_parts([loader(rows) for rows in row_parts], wpo_ref, wao_ref, wo_ref, g_ref, b_ref,
                             wple_ref, wgate_ref)
    for rows, out in zip(row_parts, outs):
        y_ref[rows, :] = out


def _full(shape):
    return pl.BlockSpec(shape, lambda *_: (0,) * len(shape))


def _tail_weight_specs():
    return [
        _full((len(POOL_WINDOWS), POOL_GROUP, POOL_GROUP)), _full((1, POOL_WIDTH)),
        _full((POOL_WIDTH, D_MODEL)), _full((ATT_WIDTH, D_MODEL)), _full((D_MODEL, D_MODEL)),
        _full((1, D_MODEL)), _full((1, D_MODEL)), _full((PLE_DIM, D_MODEL)), _full((D_MODEL, D_MODEL)),
    ]


def _prompt_tail(x, attT, zb, u, za, ga, gb, p, weights):
    b, s, _ = x.shape
    assert s % TAIL_TILE == 0
    row = lambda width: pl.BlockSpec((None, TAIL_TILE, width), lambda bi, i: (bi, i, 0))
    hist_blocks = TAIL_TILE // HIST_ROWS
    in_specs = [
        row(D_MODEL),
        pl.BlockSpec((None, ATT_WIDTH, TAIL_TILE), lambda bi, i: (bi, 0, i)),
        row(ATT_WIDTH), row(POOL_WIDTH),
        pl.BlockSpec((None, HIST_ROWS, POOL_WIDTH), lambda bi, i: (bi, jnp.maximum(i * hist_blocks - 1, 0), 0)),
        row(POOL_WIDTH), row(D_MODEL), row(D_MODEL), row(PLE_DIM),
    ] + _tail_weight_specs()
    return pl.pallas_call(
        _prompt_tail_kernel,
        grid=(b, s // TAIL_TILE),
        in_specs=in_specs,
        out_specs=row(D_MODEL),
        out_shape=jax.ShapeDtypeStruct((b, s, D_MODEL), F32),
        compiler_params=pltpu.CompilerParams(
            dimension_semantics=("arbitrary", "arbitrary"), vmem_limit_bytes=VMEM_LIMIT),
        name="prompt_tail",
    )(x, attT, zb, u, u, za, ga, gb, p, *weights)


def _sample_inproj_kernel(x_ref, w_ref, cos_ref, sin_ref,
                          q_ref, k_ref, v_ref, zb_ref, u_ref, za_ref, ga_ref, gb_ref):
    xb = x_ref[...].astype(BF16)
    cos = cos_ref[...]
    sin = sin_ref[...]
    q_ref[...] = _rope(_project(xb, w_ref, 0), cos, sin) * Q_SCALE
    k_ref[...] = _rope(_project(xb, w_ref, 1), cos, sin)
    v_ref[...] = _project(xb, w_ref, 2)
    zb_ref[...] = _project(xb, w_ref, 3)
    u_ref[...] = _project(xb, w_ref, 4)
    za_ref[...] = _project(xb, w_ref, 5)
    ga_ref[...] = _project(xb, w_ref, 6)
    gb_ref[...] = _project(xb, w_ref, 7)


def _sample_inproj(x, w_in, cos, sin):
    n = x.shape[0]
    widths = (ATT_WIDTH, ATT_WIDTH, ATT_WIDTH, ATT_WIDTH, POOL_WIDTH, POOL_WIDTH, D_MODEL, D_MODEL)
    return pl.pallas_call(
        _sample_inproj_kernel,
        out_shape=[jax.ShapeDtypeStruct((n, w), F32) for w in widths],
        compiler_params=pltpu.CompilerParams(vmem_limit_bytes=VMEM_LIMIT),
        name="sample_inproj",
    )(x, w_in, cos, sin)


def _sample_select_kernel(logit_ref, q_ref, kn_ref, psel_ref, ids_ref, den_ref, pnew_ref):
    logits = logit_ref[...]
    n_blk = logits.shape[1]
    sc = jnp.sum(logits, axis=2, keepdims=True) / MOBA_BLOCK
    blk = lax.broadcasted_iota(jnp.int32, sc.shape, 1)
    picks = []
    for t in range(MOBA_TOPK):
        m = jnp.max(sc, axis=1, keepdims=True)
        first = jnp.min(jnp.where(sc == m, blk, n_blk), axis=1, keepdims=True)
        ids_ref[:, t:t + 1, :] = jnp.broadcast_to(first, (N_HEADS, 1, LANES))
        picks.append(blk == first)
        sc = jnp.where(picks[-1], -jnp.inf, sc)
    chosen = picks[0]
    for pick in picks[1:]:
        chosen = chosen | pick
    s_new = jnp.sum(q_ref[...] * kn_ref[...], axis=1, keepdims=True)[:, :, 0:1]
    masked = jnp.where(chosen, logits, MASK_NEG)
    m = jnp.max(jnp.max(masked, axis=2, keepdims=True), axis=1, keepdims=True)
    m = jnp.maximum(m, s_new)
    p = jnp.exp2(masked - m)
    p_new = jnp.exp2(s_new - m)
    den = jnp.sum(jnp.sum(p, axis=2, keepdims=True), axis=1, keepdims=True) + p_new
    den_ref[...] = jnp.broadcast_to(den, den_ref.shape)
    pnew_ref[...] = jnp.broadcast_to(p_new, pnew_ref.shape)
    for t, pick in enumerate(picks):
        psel_ref[:, t:t + 1, :] = jnp.sum(jnp.where(pick, p, 0.0), axis=1, keepdims=True)


PV_HEADS = 8


def _sample_pv_kernel(pt_ref, ids_ref, *refs):
    per_head = MOBA_TOPK * PAGES_PER_BLOCK
    slabs = refs[:PV_HEADS * per_head]
    psel_ref, den_ref, pnew_ref, vn_ref, o_ref = refs[PV_HEADS * per_head:]
    for hh in range(PV_HEADS):
        acc = pnew_ref[hh] * vn_ref[hh] * (1.0 / PAGE_SIZE)
        for t in range(MOBA_TOPK):
            for r in range(PAGES_PER_BLOCK):
                slab = slabs[(hh * MOBA_TOPK + t) * PAGES_PER_BLOCK + r][...]
                acc = acc + slab * psel_ref[hh, t:t + 1, r * PAGE_SIZE:(r + 1) * PAGE_SIZE]
        out = jnp.sum(acc, axis=1, keepdims=True) / den_ref[hh, :, 0:1]
        o_ref[hh] = jnp.broadcast_to(out, o_ref.shape[1:])


def _sample_pv(page_table, ids, cache_vT, psel, den, pnew, v_new):
    db = page_table.shape[0]

    def slab_spec(hh, t, r):
        def index(b, hg, pt, ids):
            h = hg * PV_HEADS + hh
            blk = jnp.clip(ids[(b * N_HEADS + h) * MOBA_TOPK + t], 0, pt.shape[1] // PAGES_PER_BLOCK - 1)
            return (0, pt[b, blk * PAGES_PER_BLOCK + r], h, 0, 0)
        return pl.BlockSpec((None, None, None, HEAD_DIM, PAGE_SIZE), index)

    slab_specs = [slab_spec(hh, t, r) for hh in range(PV_HEADS) for t in range(MOBA_TOPK)
                  for r in range(PAGES_PER_BLOCK)]
    group = lambda *tail: pl.BlockSpec((None, PV_HEADS) + tail, lambda b, hg, pt, ids: (b, hg) + (0,) * len(tail))
    grid_spec = pltpu.PrefetchScalarGridSpec(
        num_scalar_prefetch=2,
        grid=(db, N_HEADS // PV_HEADS),
        in_specs=slab_specs + [group(MOBA_TOPK, MOBA_BLOCK), group(1, LANES), group(1, LANES),
                               group(HEAD_DIM, PAGE_SIZE)],
        out_specs=group(HEAD_DIM, LANES),
    )
    return pl.pallas_call(
        _sample_pv_kernel,
        grid_spec=grid_spec,
        out_shape=jax.ShapeDtypeStruct((db, N_HEADS, HEAD_DIM, LANES), F32),
        compiler_params=pltpu.CompilerParams(
            dimension_semantics=("arbitrary", "arbitrary"), vmem_limit_bytes=VMEM_LIMIT),
        name="sample_pv",
    )(page_table, ids, *([cache_vT] * len(slab_specs)), psel, den, pnew, v_new)


def _sample_tail_kernel(x_ref, att_ref, zb_ref, u_ref, state_ref, za_ref, ga_ref, gb_ref, p_ref,
                        wmix_ref, scale_ref, wpo_ref, wao_ref, wo_ref, g_ref, b_ref, wple_ref, wgate_ref,
                        y_ref):
    u = u_ref[...]
    diffs = []
    for g, w in enumerate(POOL_WINDOWS):
        lanes = slice(g * POOL_GROUP, (g + 1) * POOL_GROUP)
        total = u[:, lanes]
        for back in range(1, w):
            total = total + state_ref[POOL_STATE - back, :, lanes]
        diffs.append(total / float(w) - u[:, lanes])
    whole = lambda: dict(x=x_ref[...], att=att_ref[...], zb=zb_ref[...], za=za_ref[...],
                         pool=_pool_mix(diffs, wmix_ref, scale_ref[...]),
                         ga=ga_ref[...], gb=gb_ref[...], p=p_ref[...])
    y_ref[...] = _layer_tail_parts([whole], wpo_ref, wao_ref, wo_ref, g_ref, b_ref, wple_ref, wgate_ref)[0]


def _sample_tail(x, att, zb, u, state_t, za, ga, gb, p, weights):
    return pl.pallas_call(
        _sample_tail_kernel,
        out_shape=jax.ShapeDtypeStruct(x.shape, F32),
        compiler_params=pltpu.CompilerParams(vmem_limit_bytes=VMEM_LIMIT),
        name="sample_tail",
    )(x, att, zb, u, state_t, za, ga, gb, p, *weights)


def _rope_tables(pos):
    inv = ROPE_THETA ** (-jnp.arange(HALF_DIM, dtype=F32) / HALF_DIM)
    ang = pos.astype(F32)[:, None] * inv[None, :]
    cos = jnp.cos(ang)
    sin = jnp.sin(ang)
    return jnp.tile(cos, (1, 4)), jnp.tile(jnp.concatenate([-sin, sin], axis=1), (1, 2))


def kernel(x_prompt, x_sample, cache_k, cache_v, state_pool, page_table, p_prompt, p_sample, w_in, w_pool_mix, pool_scale, w_pool_out, w_att_out, w_o, ln_g, ln_b, w_ple, w_ple_gate):
    assert w_in.shape[0] == DEPTH
    b, s, _ = x_prompt.shape
    db, n, _ = x_sample.shape
    assert n == 1
    n_pages = page_table.shape[1]
    past_len = n_pages * PAGE_SIZE
    assert past_len % MOBA_BLOCK == 0 and past_len // MOBA_BLOCK >= MOBA_TOPK
    nb = s // MOBA_BLOCK

    w_in_b = w_in[0].astype(BF16)
    weights = (w_pool_mix[0].astype(BF16), pool_scale[0][None, :], w_pool_out[0].astype(BF16),
               w_att_out[0].astype(BF16), w_o[0].astype(BF16), ln_g[0][None, :], ln_b[0][None, :],
               w_ple[0].astype(BF16), w_ple_gate[0].astype(BF16))

    cos_s, sin_s = _rope_tables(jnp.full((1,), past_len))
    q_s, k_s, v_s, zb_s, u_s, za_s, ga_s, gb_s = _sample_inproj(x_sample[:, 0], w_in_b, cos_s, sin_s)
    cache_kT = cache_k.transpose(0, 1, 3, 4, 2)
    cache_vT = cache_v.transpose(0, 1, 3, 4, 2)
    column = lambda t: jnp.broadcast_to(t.reshape(db, N_HEADS, HEAD_DIM, 1), (db, N_HEADS, HEAD_DIM, PAGE_SIZE))

    cos_p, sin_p = _rope_tables(jnp.arange(s))
    qT, kx, vT, k_p, v_p, kmean, zb, u_p, za, ga, gb = _prompt_inproj(x_prompt, w_in_b, cos_p, sin_p)
    kmean_h = kmean.reshape(b, nb, N_HEADS, HEAD_DIM).transpose(0, 2, 1, 3)
    attT, psel, ids, den, pnew = _prompt_attn(qT, kmean_h, kx, vT, page_table, column(q_s), column(k_s), cache_kT)
    y_prompt = _prompt_tail(x_prompt, attT, zb, u_p, za, ga, gb, p_prompt[0], weights)

    att_s = _sample_pv(page_table, ids[..., 0].reshape(-1), cache_vT, psel, den, pnew, column(v_s))[..., 0]
    state = state_pool[0]
    y_sample = _sample_tail(x_sample[:, 0], att_s.reshape(db, ATT_WIDTH), zb_s, u_s, state.transpose(1, 0, 2),
                            za_s, ga_s, gb_s, p_sample[0][:, 0], weights)

    heads = (N_HEADS, HEAD_DIM)
    pool_prompt = u_p[:, s - POOL_STATE:]
    pool_sample = jnp.concatenate([state[:, 1:], u_s[:, None, :]], axis=1)
    return (y_prompt, y_sample[:, None, :],
            k_p.reshape(1, b, s, *heads), v_p.reshape(1, b, s, *heads), pool_prompt[None],
            k_s.reshape(1, db, 1, *heads), v_s.reshape(1, db, 1, *heads), pool_sample[None])
```
